```python
import math
import jax, jax.numpy as jnp
from jax import lax
import numpy as np

D_MODEL = 2048
BATCH = 4
SEQ = 2048
DEPTH = 1
DEC_BATCH = 128
DEC_SEQ = 8
PAST_LEN = 16384
PAGE_SIZE = 128

POOL_WIDTH = D_MODEL // 2
POOL_WINDOWS = (2, 4, 8, 16)
N_POOL_GROUPS = 4
POOL_GROUP = POOL_WIDTH // N_POOL_GROUPS
POOL_BUF = 15
MLSTM_HEADS = 4
MLSTM_WIDTH = D_MODEL
HEAD_DIM = MLSTM_WIDTH // MLSTM_HEADS
MLSTM_CHUNK = 64
N_EXPERT_GROUPS = 4
EXPERTS_PER_GROUP = 8
N_EXPERTS = N_EXPERT_GROUPS * EXPERTS_PER_GROUP
TOP_K_IN_GROUP = 2
EXPERT_FF = D_MODEL // 4
IN_COLS = POOL_WIDTH + 4 * MLSTM_WIDTH + 2 * MLSTM_HEADS + 2 * D_MODEL
RMS_EPS = 1e-6

kernel_name = "hybrid_pool_mlstm_hmoe_step"


def _rmsnorm(x, g):
    xf = x.astype(jnp.float32)
    xf = xf * lax.rsqrt(jnp.mean(xf * xf, axis=-1, keepdims=True) + RMS_EPS)
    return (xf * g.astype(jnp.float32)).astype(x.dtype)


def _split_in(z):
    sizes = (POOL_WIDTH, MLSTM_WIDTH, MLSTM_WIDTH, MLSTM_WIDTH, MLSTM_WIDTH,
             MLSTM_HEADS, MLSTM_HEADS, D_MODEL)
    idx = [int(i) for i in np.cumsum(sizes)]
    return jnp.split(z, idx, axis=-1)


def _pool_mixer(u, buf, pos0, w_pool, pool_scale):
    B, S, P = u.shape
    full = jnp.concatenate([buf.astype(jnp.float32), u.astype(jnp.float32)], axis=1)
    csum = jnp.concatenate([jnp.zeros((B, 1, P), jnp.float32), jnp.cumsum(full, axis=1)], axis=1)
    uf = u.astype(jnp.float32)
    outs = []
    for g, w in enumerate(POOL_WINDOWS):
        cs = csum[..., g * POOL_GROUP:(g + 1) * POOL_GROUP]
        wsum = cs[:, POOL_BUF + 1:POOL_BUF + 1 + S] - cs[:, POOL_BUF + 1 - w:POOL_BUF + 1 - w + S]
        count = jnp.minimum(pos0 + jnp.arange(S, dtype=jnp.int32) + 1, w).astype(jnp.float32)[None, :, None]
        outs.append(wsum / count - uf[..., g * POOL_GROUP:(g + 1) * POOL_GROUP])
    pooled = jnp.stack(outs, axis=2)
    mixed = jnp.einsum('bsgc,gcd->bsgd', pooled, w_pool.astype(jnp.float32)).reshape(B, S, P)
    out = (mixed * pool_scale.astype(jnp.float32)).astype(u.dtype)
    return out, full[:, -POOL_BUF:].astype(u.dtype)


def _mlstm_chunk(carry, inp):
    C, n, m = carry
    q, k, v, ig, lf = inp
    L = q.shape[2]
    b = jnp.cumsum(lf, axis=-1)
    causal = jnp.tril(jnp.ones((L, L), dtype=bool))
    dmat = jnp.where(causal, b[..., :, None] - b[..., None, :] + ig[..., None, :], -jnp.inf)
    inter = b + m[..., None]
    m_new = jnp.maximum(inter, jnp.max(dmat, axis=-1))
    w_intra = jnp.exp(dmat - m_new[..., None])
    w_inter = jnp.exp(inter - m_new)
    s = jnp.einsum('bhtd,bhsd->bhts', q, k) * w_intra
    num = jnp.einsum('bhts,bhsv->bhtv', s, v) + w_inter[..., None] * jnp.einsum('bhtd,bhdv->bhtv', q, C)
    den = jnp.sum(s, axis=-1) + w_inter * jnp.einsum('bhtd,bhd->bht', q, n)
    h = num / jnp.maximum(jnp.abs(den), jnp.exp(-m_new))[..., None]
    m_last = m_new[..., -1]
    wl_inter = jnp.exp(b[..., -1] + m - m_last)
    wl = jnp.exp(b[..., -1:] - b + ig - m_last[..., None])
    C_new = wl_inter[..., None, None] * C + jnp.einsum('bhs,bhsd,bhsv->bhdv', wl, k, v)
    n_new = wl_inter[..., None] * n + jnp.einsum('bhs,bhsd->bhd', wl, k)
    return (C_new, n_new, m_last), h


def _mlstm(q, k, v, ig, lf, C0, n0, m0):
    B, H, S, d = q.shape
    L = math.gcd(S, MLSTM_CHUNK)
    nc = S // L

    def to_blocks(a):
        return jnp.moveaxis(a.reshape(a.shape[:2] + (nc, L) + a.shape[3:]), 2, 0)

    carry0 = (C0.astype(jnp.float32), n0.astype(jnp.float32), m0.astype(jnp.float32))
    (C, n, m), h = lax.scan(_mlstm_chunk, carry0, tuple(to_blocks(a) for a in (q, k, v, ig, lf)))
    h = jnp.moveaxis(h, 0, 2).reshape(B, H, S, d)
    return h, C, n, m


def _hier_moe(x, w_rg, b_rg, w_re, b_re, w_eg, w_eu, w_ed):
    B, S, D = x.shape
    xt = x.reshape(B * S, D)
    T = xt.shape[0]
    g_logits = (xt @ w_rg).astype(jnp.float32) + b_rg.astype(jnp.float32)
    g_prob = jax.nn.softmax(g_logits, axis=-1)
    g_idx = jnp.argmax(g_logits, axis=-1)
    g_val = jnp.take_along_axis(g_prob, g_idx[:, None], axis=-1)
    e_logits = ((xt @ w_re).astype(jnp.float32) + b_re.astype(jnp.float32)).reshape(T, N_EXPERT_GROUPS, EXPERTS_PER_GROUP)
    in_group = e_logits[jnp.arange(T), g_idx]
    top_val, top_idx = lax.top_k(in_group, TOP_K_IN_GROUP)
    top_w = jax.nn.softmax(top_val, axis=-1) * g_val
    expert_id = g_idx[:, None] * EXPERTS_PER_GROUP + top_idx
    combine = jnp.einsum('tk,tke->te', top_w, jax.nn.one_hot(expert_id, N_EXPERTS, dtype=jnp.float32))
    hg = jnp.einsum('td,edf->tef', xt, w_eg)
    hu = jnp.einsum('td,edf->tef', xt, w_eu)
    h = jax.nn.silu(hg) * hu * combine[..., None].astype(x.dtype)
    return jnp.einsum('tef,efd->td', h, w_ed).reshape(B, S, D)


def _layer(x, buf, C0, n0, m0, pos0, g_mix, w_in, b_if, w_pool, pool_scale, w_proj_a, w_proj_b,
           g_head, w_out, g_ffn, w_rg, b_rg, w_re, b_re, w_eg, w_eu, w_ed):
    B, S, _ = x.shape
    hN = _rmsnorm(x, g_mix)
    z = hN @ w_in
    u, q, k, v, o, ig, fg, ga, gb = _split_in(z)
    a, new_buf = _pool_mixer(u, buf, pos0, w_pool, pool_scale)
    def heads(t):
        return jnp.transpose(t.reshape(B, S, MLSTM_HEADS, HEAD_DIM), (0, 2, 1, 3)).astype(jnp.float32)
    qh, kh, vh = heads(q), heads(k) * (HEAD_DIM ** -0.5), heads(v)
    ig_pre = jnp.transpose(ig.astype(jnp.float32) + b_if[:MLSTM_HEADS].astype(jnp.float32), (0, 2, 1))
    lf = jax.nn.log_sigmoid(jnp.transpose(fg.astype(jnp.float32) + b_if[MLSTM_HEADS:].astype(jnp.float32), (0, 2, 1)))
    h, C, n, m = _mlstm(qh, kh, vh, ig_pre, lf, C0, n0, m0)
    h = jnp.transpose(h, (0, 2, 1, 3))
    h = h * lax.rsqrt(jnp.mean(h * h, axis=-1, keepdims=True) + RMS_EPS)
    h = h * g_head.astype(jnp.float32).reshape(MLSTM_HEADS, HEAD_DIM)
    hb = (h.reshape(B, S, MLSTM_WIDTH) * jax.nn.sigmoid(o.astype(jnp.float32))).astype(x.dtype)
    mix = jax.nn.sigmoid(ga) * (a @ w_proj_a) + jax.nn.sigmoid(gb) * (hb @ w_proj_b)
    x = x + mix @ w_out
    x = x + _hier_moe(_rmsnorm(x, g_ffn), w_rg, b_rg, w_re, b_re, w_eg, w_eu, w_ed)
    return x, new_buf, C, n, m


def setup_inputs(seed: int = 0) -> dict:
    key = jax.random.key(seed)
    ks = jax.random.split(key, 32)
    f32 = jnp.float32
    nrm = lambda k, s, sc: jax.random.normal(k, s, f32) * sc
    H = MLSTM_HEADS
    b_i = nrm(ks[0], (DEPTH, H), 0.1)
    b_f = jnp.broadcast_to(jnp.linspace(3.0, 6.0, H, dtype=f32), (DEPTH, H)) + nrm(ks[1], (DEPTH, H), 0.1)
    return {
        "x_prompt": nrm(ks[2], (BATCH, SEQ, D_MODEL), 1.0),
        "x_sample": nrm(ks[3], (DEC_BATCH, DEC_SEQ, D_MODEL), 1.0),
        "state_pool": nrm(ks[4], (DEPTH, DEC_BATCH, POOL_BUF, POOL_WIDTH), 1.0),
        "state_C": nrm(ks[5], (DEPTH, DEC_BATCH, H, HEAD_DIM, HEAD_DIM), 0.05),
        "state_n": nrm(ks[6], (DEPTH, DEC_BATCH, H, HEAD_DIM), 0.05),
        "state_m": nrm(ks[7], (DEPTH, DEC_BATCH, H), 1.0),
        "g_mix": 1.0 + nrm(ks[8], (DEPTH, D_MODEL), 0.02),
        "w_in": nrm(ks[9], (DEPTH, D_MODEL, IN_COLS), D_MODEL ** -0.5),
        "b_if": jnp.concatenate([b_i, b_f], axis=-1),
        "w_pool": nrm(ks[10], (DEPTH, N_POOL_GROUPS, POOL_GROUP, POOL_GROUP), POOL_GROUP ** -0.5),
        "pool_scale": 1.0 + nrm(ks[11], (DEPTH, POOL_WIDTH), 0.02),
        "w_proj_a": nrm(ks[12], (DEPTH, POOL_WIDTH, D_MODEL), POOL_WIDTH ** -0.5),
        "w_proj_b": nrm(ks[13], (DEPTH, MLSTM_WIDTH, D_MODEL), MLSTM_WIDTH ** -0.5),
        "g_head": 1.0 + nrm(ks[14], (DEPTH, MLSTM_WIDTH), 0.02),
        "w_out": nrm(ks[15], (DEPTH, D_MODEL, D_MODEL), D_MODEL ** -0.5),
        "g_ffn": 1.0 + nrm(ks[16], (DEPTH, D_MODEL), 0.02),
        "w_router_group": nrm(ks[17], (DEPTH, D_MODEL, N_EXPERT_GROUPS), D_MODEL ** -0.5),
        "b_router_group": nrm(ks[18], (DEPTH, N_EXPERT_GROUPS), 0.01),
        "w_router_expert": nrm(ks[19], (DEPTH, D_MODEL, N_EXPERTS), D_MODEL ** -0.5),
        "b_router_expert": nrm(ks[20], (DEPTH, N_EXPERTS), 0.01),
        "w_exp_gate": nrm(ks[21], (DEPTH, N_EXPERTS, D_MODEL, EXPERT_FF), D_MODEL ** -0.5),
        "w_exp_up": nrm(ks[22], (DEPTH, N_EXPERTS, D_MODEL, EXPERT_FF), D_MODEL ** -0.5),
        "w_exp_down": nrm(ks[23], (DEPTH, N_EXPERTS, EXPERT_FF, D_MODEL), EXPERT_FF ** -0.5),
        "g_final": 1.0 + nrm(ks[24], (D_MODEL,), 0.02),
    }


def reference(x_prompt, x_sample, state_pool, state_C, state_n, state_m, g_mix, w_in, b_if, w_pool,
              pool_scale, w_proj_a, w_proj_b, g_head, w_out, g_ffn, w_router_group, b_router_group,
              w_router_expert, b_router_expert, w_exp_gate, w_exp_up, w_exp_down, g_final):
    yp, ys = x_prompt, x_sample
    B = x_prompt.shape[0]
    pool_p, C_p, n_p, m_p = [], [], [], []
    pool_s, C_s, n_s, m_s = [], [], [], []
    for l in range(DEPTH):
        params = (g_mix[l], w_in[l], b_if[l], w_pool[l], pool_scale[l], w_proj_a[l], w_proj_b[l],
                  g_head[l], w_out[l], g_ffn[l], w_router_group[l], b_router_group[l],
                  w_router_expert[l], b_router_expert[l], w_exp_gate[l], w_exp_up[l], w_exp_down[l])
        buf0 = jnp.zeros((B, POOL_BUF, POOL_WIDTH), yp.dtype)
        C0 = jnp.zeros((B, MLSTM_HEADS, HEAD_DIM, HEAD_DIM), jnp.float32)
        n0 = jnp.zeros((B, MLSTM_HEADS, HEAD_DIM), jnp.float32)
        m0 = jnp.zeros((B, MLSTM_HEADS), jnp.float32)
        yp, bp, cp, nq, mq = _layer(yp, buf0, C0, n0, m0, 0, *params)
        ys, bs, cs, ns, ms = _layer(ys, state_pool[l], state_C[l], state_n[l], state_m[l], PAST_LEN, *params)
        pool_p.append(bp); C_p.append(cp); n_p.append(nq); m_p.append(mq)
        pool_s.append(bs); C_s.append(cs); n_s.append(ns); m_s.append(ms)
    y_prompt = _rmsnorm(yp, g_final)
    y_sample = _rmsnorm(ys, g_final)
    return (y_prompt, y_sample, jnp.stack(pool_p), jnp.stack(C_p), jnp.stack(n_p), jnp.stack(m_p),
            jnp.stack(pool_s), jnp.stack(C_s), jnp.stack(n_s), jnp.stack(m_s))
```

```python
import functools
import math

import jax
import jax.numpy as jnp
from jax import lax
from jax.experimental import pallas as pl
from jax.experimental.pallas import tpu as pltpu

F32 = jnp.float32
BF16 = jnp.bfloat16
HIGHEST = lax.Precision.HIGHEST

RMS_EPS = 1e-6
POOL_WINDOWS = (2, 4, 8, 16)
POOL_HIST = 16
SAMPLE_POS0 = 16384
N_EXPERT_GROUPS = 4
EXPERTS_PER_GROUP = 8
N_EXPERTS = N_EXPERT_GROUPS * EXPERTS_PER_GROUP
LANES = 128
VMEM_LIMIT = 56 * 1024 * 1024


def _cparams(n_axes):
    return pltpu.CompilerParams(dimension_semantics=("arbitrary",) * n_axes,
                                vmem_limit_bytes=VMEM_LIMIT)


def _tile(n, pref):
    t = min(n, pref)
    while n % t:
        t -= 1
    return t


def _rms_proj_kernel(x_ref, g_ref, w_ref, xn_ref, p_ref):
    x = x_ref[...]
    xn = x * lax.rsqrt(jnp.mean(x * x, axis=-1, keepdims=True) + RMS_EPS) * g_ref[...]
    xn_ref[...] = xn.astype(xn_ref.dtype)
    p_ref[...] = jnp.dot(xn, w_ref[...], precision=HIGHEST, preferred_element_type=F32)


def _rms_proj(x, g, w_small, xn_dtype, tm):
    T, D = x.shape
    tm = _tile(T, tm)
    return pl.pallas_call(
        _rms_proj_kernel,
        grid=(T // tm,),
        in_specs=[pl.BlockSpec((tm, D), lambda i: (i, 0)),
                  pl.BlockSpec((1, D), lambda i: (0, 0)),
                  pl.BlockSpec((D, LANES), lambda i: (0, 0))],
        out_specs=[pl.BlockSpec((tm, D), lambda i: (i, 0)),
                   pl.BlockSpec((tm, LANES), lambda i: (i, 0))],
        out_shape=[jax.ShapeDtypeStruct((T, D), xn_dtype),
                   jax.ShapeDtypeStruct((T, LANES), F32)],
        compiler_params=_cparams(1),
        name="rms_proj",
    )(x, g.reshape(1, D), w_small)


def _mm_kernel(*refs, has_scale, has_res):
    x_ref, w_ref = refs[0], refs[1]
    k = 2
    scale_ref = res_ref = None
    if has_scale:
        scale_ref = refs[k]; k += 1
    if has_res:
        res_ref = refs[k]; k += 1
    o_ref, wbf_ref = refs[k], refs[k + 1]

    @pl.when(pl.program_id(1) == 0)
    def _():
        wbf_ref[...] = w_ref[...].astype(BF16)

    acc = jnp.dot(x_ref[...], wbf_ref[...], preferred_element_type=F32)
    if has_scale:
        acc = acc * scale_ref[...]
    if has_res:
        acc = acc + res_ref[...]
    o_ref[...] = acc.astype(o_ref.dtype)


def _matmul(x, w, col0, ncols, out_dtype, scale=None, res=None, tm=512, tn=1024):
    T, K = x.shape
    tm = _tile(T, tm)
    tn = _tile(math.gcd(ncols, col0) if col0 else ncols, tn)
    c0 = col0 // tn
    in_specs = [pl.BlockSpec((tm, K), lambda n, m: (m, 0)),
                pl.BlockSpec((K, tn), lambda n, m: (0, c0 + n))]
    args = [x, w]
    if scale is not None:
        in_specs.append(pl.BlockSpec((1, tn), lambda n, m: (0, n)))
        args.append(scale.reshape(1, ncols))
    if res is not None:
        in_specs.append(pl.BlockSpec((tm, tn), lambda n, m: (m, n)))
        args.append(res)
    return pl.pallas_call(
        functools.partial(_mm_kernel, has_scale=scale is not None, has_res=res is not None),
        grid=(ncols // tn, T // tm),
        in_specs=in_specs,
        out_specs=pl.BlockSpec((tm, tn), lambda n, m: (m, n)),
        out_shape=jax.ShapeDtypeStruct((T, ncols), out_dtype),
        scratch_shapes=[pltpu.VMEM((K, tn), BF16)],
        compiler_params=_cparams(2),
        name="mm_wstat",
    )(*args)


def _pool_prompt_kernel(u_ref, wp_ref, ps_ref, a_ref, ext_ref, *, ts, gc, pos0):
    s = pl.program_id(1)

    @pl.when(s == 0)
    def _():
        ext_ref[0:POOL_HIST, :] = jnp.zeros((POOL_HIST, ext_ref.shape[1]), F32)

    ext_ref[POOL_HIST:, :] = u_ref[...]
    t_abs = pos0 + s * ts + lax.broadcasted_iota(jnp.int32, (ts, 1), 0)
    for g, w in enumerate(POOL_WINDOWS):
        cols = slice(g * gc, (g + 1) * gc)
        e = ext_ref[:, cols]
        acc = e
        span = 1
        while span < w:
            acc = acc + pltpu.roll(acc, span, 0)
            span *= 2
        inv = 1.0 / jnp.minimum(t_abs + 1, w).astype(F32)
        pooled = acc[POOL_HIST:, :] * inv - e[POOL_HIST:, :]
        mixed = jnp.dot(pooled.astype(BF16), wp_ref[g], preferred_element_type=F32)
        a_ref[:, cols] = (mixed * ps_ref[:, cols]).astype(a_ref.dtype)
    ext_ref[0:POOL_HIST, :] = ext_ref[ts:ts + POOL_HIST, :]


def _pool_prompt(u_all, wp_bf, pool_scale, B, S, ts=512):
    P = u_all.shape[1]
    ts = _tile(S, ts)
    nst = S // ts
    G = len(POOL_WINDOWS)
    gc = P // G
    return pl.pallas_call(
        functools.partial(_pool_prompt_kernel, ts=ts, gc=gc, pos0=0),
        grid=(B, nst),
        in_specs=[pl.BlockSpec((ts, P), lambda b, s: (b * nst + s, 0)),
                  pl.BlockSpec((G, gc, gc), lambda b, s: (0, 0, 0)),
                  pl.BlockSpec((1, P), lambda b, s: (0, 0))],
        out_specs=pl.BlockSpec((ts, P), lambda b, s: (b * nst + s, 0)),
        out_shape=jax.ShapeDtypeStruct((B * S, P), BF16),
        scratch_shapes=[pltpu.VMEM((POOL_HIST + ts, P), F32)],
        compiler_params=_cparams(2),
        name="pool_prompt",
    )(u_all, wp_bf, pool_scale.reshape(1, P))


def _pool_sample_kernel(hist_ref, u_ref, wp_ref, ps_ref, a_ref, ext_ref, *, bb, sq, gc, pos0):
    ext_ref[:, 0:POOL_HIST, :] = hist_ref[...]
    ext_ref[:, POOL_HIST:, :] = u_ref[...]
    t_abs = pos0 + lax.broadcasted_iota(jnp.int32, (1, sq, 1), 1)
    for g, w in enumerate(POOL_WINDOWS):
        cols = slice(g * gc, (g + 1) * gc)
        cur = ext_ref[:, POOL_HIST:, cols]
        acc = cur
        for j in range(1, w):
            acc = acc + ext_ref[:, POOL_HIST - j:POOL_HIST - j + sq, cols]
        inv = 1.0 / jnp.minimum(t_abs + 1, w).astype(F32)
        pooled = (acc * inv - cur).reshape(bb * sq, gc)
        mixed = jnp.dot(pooled.astype(BF16), wp_ref[g], preferred_element_type=F32)
        a_ref[:, cols] = (mixed * ps_ref[:, cols]).astype(a_ref.dtype)


def _pool_sample(u_all, hist, wp_bf, pool_scale, row0, Bs, sq, bb=16):
    T, P = u_all.shape
    bb = _tile(Bs, bb)
    G = len(POOL_WINDOWS)
    gc = P // G
    u3 = u_all.reshape(T // sq, sq, P)
    blk0 = row0 // (sq * bb)
    return pl.pallas_call(
        functools.partial(_pool_sample_kernel, bb=bb, sq=sq, gc=gc, pos0=SAMPLE_POS0),
        grid=(Bs // bb,),
        in_specs=[pl.BlockSpec((bb, POOL_HIST, P), lambda i: (i, 0, 0)),
                  pl.BlockSpec((bb, sq, P), lambda i: (blk0 + i, 0, 0)),
                  pl.BlockSpec((G, gc, gc), lambda i: (0, 0, 0)),
                  pl.BlockSpec((1, P), lambda i: (0, 0))],
        out_specs=pl.BlockSpec((bb * sq, P), lambda i: (i, 0)),
        out_shape=jax.ShapeDtypeStruct((Bs * sq, P), BF16),
        scratch_shapes=[pltpu.VMEM((bb, POOL_HIST + sq, P), F32)],
        compiler_params=_cparams(1),
        name="pool_sample",
    )(hist, u3, wp_bf, pool_scale.reshape(1, P))


def _log_sigmoid(x):
    return jnp.minimum(x, 0.0) - jnp.log1p(jnp.exp(-jnp.abs(x)))


def _mlstm_cell(q, k, v, ig_c, ig_r, lf_c, lf_r, states, ls):
    L = q.shape[0]
    G = len(states)
    row = lax.broadcasted_iota(jnp.int32, (L, L), 0)
    col = lax.broadcasted_iota(jnp.int32, (L, L), 1)
    lower = col <= row
    upper = row <= col
    seg_c = None
    if G > 1:
        same = (row // ls) == (col // ls)
        lower = lower & same
        upper = upper & same
        seg_c = lax.broadcasted_iota(jnp.int32, (L, 1), 0) // ls

    def per_seq(vals):
        out = vals[0]
        for g in range(1, G):
            out = jnp.where(seg_c == g, vals[g], out)
        return out

    b_c = jnp.sum(jnp.where(lower, lf_r, 0.0), axis=1, keepdims=True)
    b_r = jnp.sum(jnp.where(upper, lf_c, 0.0), axis=0, keepdims=True)
    dmat = jnp.where(lower, b_c - b_r + ig_r, -jnp.inf)
    inter = b_c + per_seq([st[2] for st in states])
    m_new = jnp.maximum(inter, jnp.max(dmat, axis=1, keepdims=True))
    w_intra = jnp.exp(dmat - m_new)
    w_inter = jnp.exp(inter - m_new)
    s = lax.dot_general(q, k, (((1,), (1,)), ((), ())), preferred_element_type=F32) * w_intra
    qf = q.astype(F32)
    qc = [jnp.dot(q, st[0].astype(BF16), preferred_element_type=F32) for st in states]
    qn = [jnp.sum(qf * st[1], axis=1, keepdims=True) for st in states]
    if G > 1:
        q_c = qc[0]
        for g in range(1, G):
            q_c = jnp.where(seg_c == g, qc[g], q_c)
        q_n = per_seq(qn)
    else:
        q_c, q_n = qc[0], qn[0]
    num = jnp.dot(s.astype(BF16), v, preferred_element_type=F32) + w_inter * q_c
    den = jnp.sum(s, axis=1, keepdims=True) + w_inter * q_n
    h = num / jnp.maximum(jnp.abs(den), jnp.exp(-m_new))

    m_last = [m_new[(g + 1) * ls - 1:(g + 1) * ls, :] for g in range(G)]
    b_last = [b_c[(g + 1) * ls - 1:(g + 1) * ls, :] for g in range(G)]
    wl_c = jnp.exp(per_seq(b_last) - b_c + ig_c - per_seq(m_last))
    wk = k.astype(F32) * wl_c
    new_states = []
    for g, (C, n, m) in enumerate(states):
        wk_g = wk if G == 1 else jnp.where(seg_c == g, wk, 0.0)
        wl_inter = jnp.exp(b_last[g] + m - m_last[g])
        kv = lax.dot_general(wk_g.astype(BF16), v, (((0,), (0,)), ((), ())),
                             preferred_element_type=F32)
        new_states.append((wl_inter * C + kv,
                           wl_inter * n + jnp.sum(wk_g, axis=0, keepdims=True),
                           m_last[g]))
    return h, new_states


def _head_out(h, gh, o):
    hn = h * lax.rsqrt(jnp.mean(h * h, axis=-1, keepdims=True) + RMS_EPS) * gh
    return hn * jax.nn.sigmoid(o.astype(F32))


def _gate_cols(g_c, g_r, bc_ref, br_ref):
    pre_c = g_c + br_ref[...]
    pre_r = g_r + bc_ref[...]
    return pre_c, _log_sigmoid(pre_c), pre_r, _log_sigmoid(pre_r)


def _mlstm_prompt_kernel(q_ref, k_ref, v_ref, o_ref, gc_ref, gr_ref, bc_ref, br_ref, gh_ref,
                         hb_ref, C_out, n_out, m_out, C_s, n_s, m_s, *, nh, d):
    c = pl.program_id(1)

    @pl.when(c == 0)
    def _():
        C_s[...] = jnp.zeros(C_s.shape, F32)
        n_s[...] = jnp.zeros(n_s.shape, F32)
        m_s[...] = jnp.zeros(m_s.shape, F32)

    L = q_ref.shape[0]
    pre_c, lf_call, pre_r, lf_rall = _gate_cols(gc_ref[...], gr_ref[...], bc_ref, br_ref)
    for hd in range(nh):
        cols = slice(hd * d, (hd + 1) * d)
        st = (C_s[hd], n_s[hd], m_s[hd])
        h, (new,) = _mlstm_cell(
            q_ref[:, cols], k_ref[:, cols], v_ref[:, cols],
            pre_c[:, hd:hd + 1], pre_r[hd:hd + 1, :],
            lf_call[:, nh + hd:nh + hd + 1], lf_rall[nh + hd:nh + hd + 1, :],
            [st], L)
        C_s[hd], n_s[hd], m_s[hd] = new
        hb_ref[:, cols] = _head_out(h, gh_ref[:, cols], o_ref[:, cols]).astype(hb_ref.dtype)

    @pl.when(c == pl.num_programs(1) - 1)
    def _():
        C_out[0] = C_s[...]
        n_out[0] = n_s[...]
        m_out[0] = m_s[...]


def _mlstm_prompt(qkvo, gates_c, gates_r, b_if, g_head, B, S, nh, chunk=256):
    W = qkvo.shape[1] // 4
    d = W // nh
    L = _tile(S, chunk)
    nc = S // L
    tok = lambda j: pl.BlockSpec((L, W), lambda b, c, j=j: (b * nc + c, j))
    const = lambda shape: pl.BlockSpec(shape, lambda b, c: (0,) * len(shape))
    return pl.pallas_call(
        functools.partial(_mlstm_prompt_kernel, nh=nh, d=d),
        grid=(B, nc),
        in_specs=[tok(0), tok(1), tok(2), tok(3),
                  pl.BlockSpec((L, 2 * nh), lambda b, c: (b * nc + c, 0)),
                  pl.BlockSpec((2 * nh, L), lambda b, c: (0, b * nc + c)),
                  const((2 * nh, 1)), const((1, 2 * nh)), const((1, W))],
        out_specs=[pl.BlockSpec((L, W), lambda b, c: (b * nc + c, 0)),
                   pl.BlockSpec((1, nh, d, d), lambda b, c: (b, 0, 0, 0)),
                   pl.BlockSpec((1, nh, 1, d), lambda b, c: (b, 0, 0, 0)),
                   pl.BlockSpec((1, nh, 1, 1), lambda b, c: (b, 0, 0, 0))],
        out_shape=[jax.ShapeDtypeStruct((B * S, W), BF16),
                   jax.ShapeDtypeStruct((B, nh, d, d), F32),
                   jax.ShapeDtypeStruct((B, nh, 1, d), F32),
                   jax.ShapeDtypeStruct((B, nh, 1, 1), F32)],
        scratch_shapes=[pltpu.VMEM((nh, d, d), F32), pltpu.VMEM((nh, 1, d), F32),
                        pltpu.VMEM((nh, 1, 1), F32)],
        compiler_params=_cparams(2),
        name="mlstm_prompt",
    )(qkvo, qkvo, qkvo, qkvo, gates_c, gates_r, b_if.reshape(2 * nh, 1), b_if.reshape(1, 2 * nh),
      g_head.reshape(1, W))


def _mlstm_sample_kernel(q_ref, k_ref, v_ref, o_ref, gc_ref, gr_ref, bc_ref, br_ref, gh_ref,
                         C_in, n_in, m_in, hb_ref, C_out, n_out, m_out, *, nh, d, bb, sq):
    pre_c, lf_call, pre_r, lf_rall = _gate_cols(gc_ref[...], gr_ref[0], bc_ref, br_ref)
    for hd in range(nh):
        cols = slice(hd * d, (hd + 1) * d)
        states =[(C_in[j, hd], n_in[j, hd], m_in[j, hd]) for j in range(bb)]
        h, new = _mlstm_cell(
            q_ref[:, cols], k_ref[:, cols], v_ref[:, cols],
            pre_c[:, hd:hd + 1], pre_r[hd:hd + 1, :],
            lf_call[:, nh + hd:nh + hd + 1], lf_rall[nh + hd:nh + hd + 1, :],
            states, sq)
        for j in range(bb):
            C_out[j, hd], n_out[j, hd], m_out[j, hd] = new[j]
        hb_ref[:, cols] = _head_out(h, gh_ref[:, cols], o_ref[:, cols]).astype(hb_ref.dtype)


def _mlstm_sample(qkvo, gates_c, gates_r, b_if, g_head, C0, n0, m0, row0, Bs, sq, nh, bb=2):
    W = qkvo.shape[1] // 4
    d = W // nh
    L = bb * sq
    blk0 = row0 // L
    tok = lambda j: pl.BlockSpec((L, W), lambda i, j=j: (blk0 + i, j))
    const = lambda shape: pl.BlockSpec(shape, lambda i: (0,) * len(shape))
    st = lambda a, b: pl.BlockSpec((bb, nh, a, b), lambda i: (i, 0, 0, 0))
    return pl.pallas_call(
        functools.partial(_mlstm_sample_kernel, nh=nh, d=d, bb=bb, sq=sq),
        grid=(Bs // bb,),
        in_specs=[tok(0), tok(1), tok(2), tok(3),
                  pl.BlockSpec((L, 2 * nh), lambda i: (blk0 + i, 0)),
                  pl.BlockSpec((1, 2 * nh, L), lambda i: (i, 0, 0)),
                  const((2 * nh, 1)), const((1, 2 * nh)), const((1, W)),
                  st(d, d), st(1, d), st(1, 1)],
        out_specs=[pl.BlockSpec((L, W), lambda i: (i, 0)), st(d, d), st(1, d), st(1, 1)],
        out_shape=[jax.ShapeDtypeStruct((Bs * sq, W), BF16),
                   jax.ShapeDtypeStruct((Bs, nh, d, d), F32),
                   jax.ShapeDtypeStruct((Bs, nh, 1, d), F32),
                   jax.ShapeDtypeStruct((Bs, nh, 1, 1), F32)],
        compiler_params=_cparams(1),
        name="mlstm_sample",
    )(qkvo, qkvo, qkvo, qkvo, gates_c, gates_r, b_if.reshape(2 * nh, 1), b_if.reshape(1, 2 * nh),
      g_head.reshape(1, W), C0, n0.reshape(Bs, nh, 1, d), m0.reshape(Bs, nh, 1, 1))


def _merge_kernel(a_ref, hb_ref, ga_ref, gb_ref, wa_ref, wb_ref, o_ref, wa_bf, wb_bf):
    @pl.when(pl.program_id(1) == 0)
    def _():
        wa_bf[...] = wa_ref[...].astype(BF16)
        wb_bf[...] = wb_ref[...].astype(BF16)

    pa = jnp.dot(a_ref[...], wa_bf[...], preferred_element_type=F32)
    pb = jnp.dot(hb_ref[...], wb_bf[...], preferred_element_type=F32)
    mix = (jax.nn.sigmoid(ga_ref[...].astype(F32)) * pa
           + jax.nn.sigmoid(gb_ref[...].astype(F32)) * pb)
    o_ref[...] = mix.astype(o_ref.dtype)


def _merge(a, hb, gab, wa, wb, tm=512, tn=512):
    T, P = a.shape
    W = hb.shape[1]
    D = wa.shape[1]
    tm = _tile(T, tm)
    tn = _tile(D, tn)
    nn = D // tn
    return pl.pallas_call(
        _merge_kernel,
        grid=(nn, T // tm),
        in_specs=[pl.BlockSpec((tm, P), lambda n, m: (m, 0)),
                  pl.BlockSpec((tm, W), lambda n, m: (m, 0)),
                  pl.BlockSpec((tm, tn), lambda n, m: (m, n)),
                  pl.BlockSpec((tm, tn), lambda n, m: (m, nn + n)),
                  pl.BlockSpec((P, tn), lambda n, m: (0, n)),
                  pl.BlockSpec((W, tn), lambda n, m: (0, n))],
        out_specs=pl.BlockSpec((tm, tn), lambda n, m: (m, n)),
        out_shape=jax.ShapeDtypeStruct((T, D), BF16),
        scratch_shapes=[pltpu.VMEM((P, tn), BF16), pltpu.VMEM((W, tn), BF16)],
        compiler_params=_cparams(2),
        name="merge",
    )(a, hb, gab, gab, wa, wb)


def _router_kernel(x_ref, g_ref, w_ref, b_ref, xn_ref, id_ref, wt_ref):
    x = x_ref[...]
    xn = x * lax.rsqrt(jnp.mean(x * x, axis=-1, keepdims=True) + RMS_EPS) * g_ref[...]
    xn_ref[...] = xn
    logits = jnp.dot(xn, w_ref[...], precision=HIGHEST, preferred_element_type=F32) + b_ref[...]
    lane = lax.broadcasted_iota(jnp.int32, logits.shape, 1).astype(F32)
    neg = -jnp.inf
    gl = jnp.where(lane < N_EXPERT_GROUPS, logits, neg)
    gmax = jnp.max(gl, axis=1, keepdims=True)
    g_idx = jnp.min(jnp.where(gl == gmax, lane, float(LANES)), axis=1, keepdims=True)
    g_val = 1.0 / jnp.sum(jnp.exp(gl - gmax), axis=1, keepdims=True)
    lo = N_EXPERT_GROUPS + g_idx * EXPERTS_PER_GROUP
    el = jnp.where((lane >= lo) & (lane < lo + EXPERTS_PER_GROUP), logits, neg)
    t1 = jnp.max(el, axis=1, keepdims=True)
    i1 = jnp.min(jnp.where(el == t1, lane, float(LANES)), axis=1, keepdims=True)
    el2 = jnp.where(lane == i1, neg, el)
    t2 = jnp.max(el2, axis=1, keepdims=True)
    i2 = jnp.min(jnp.where(el2 == t2, lane, float(LANES)), axis=1, keepdims=True)
    e2 = jnp.exp(t2 - t1)
    w1 = g_val / (1.0 + e2)
    w2 = g_val * e2 / (1.0 + e2)
    ids = jnp.where(lane == 0.0, i1 - N_EXPERT_GROUPS,
                    jnp.where(lane == 1.0, i2 - N_EXPERT_GROUPS, 0.0))
    id_ref[...] = ids.astype(jnp.int32)
    wt_ref[...] = jnp.where(lane == 0.0, w1, jnp.where(lane == 1.0, w2, 0.0))


def _router(x1, g_ffn, w_r, b_r, tm=256):
    T, D = x1.shape
    tm = _tile(T, tm)
    return pl.pallas_call(
        _router_kernel,
        grid=(T // tm,),
        in_specs=[pl.BlockSpec((tm, D), lambda i: (i, 0)),
                  pl.BlockSpec((1, D), lambda i: (0, 0)),
                  pl.BlockSpec((D, LANES), lambda i: (0, 0)),
                  pl.BlockSpec((1, LANES), lambda i: (0, 0))],
        out_specs=[pl.BlockSpec((tm, D), lambda i: (i, 0)),
                   pl.BlockSpec((tm, LANES), lambda i: (i, 0)),
                   pl.BlockSpec((tm, LANES), lambda i: (i, 0))],
        out_shape=[jax.ShapeDtypeStruct((T, D), F32),
                   jax.ShapeDtypeStruct((T, LANES), jnp.int32),
                   jax.ShapeDtypeStruct((T, LANES), F32)],
        compiler_params=_cparams(1),
        name="router",
    )(x1, g_ffn.reshape(1, D), w_r, b_r)


def _row_copy(src_hbm, dst_ref, src_row, dst_row, sem):
    return pltpu.make_async_copy(src_hbm.at[pl.ds(src_row, 1)], dst_ref.at[pl.ds(dst_row, 1)], sem)


def _gather_kernel(tok_ref, nv_ref, x_hbm, o_ref, sem, *, tr):
    i = pl.program_id(0)
    nv = nv_ref[i]
    o_ref[...] = jnp.zeros(o_ref.shape, o_ref.dtype)

    def start(r, c):
        _row_copy(x_hbm, o_ref, tok_ref[i * tr + r], r, sem).start()
        return c

    def wait(r, c):
        _row_copy(x_hbm, o_ref, 0, r, sem).wait()
        return c

    lax.fori_loop(0, nv, start, 0)
    lax.fori_loop(0, nv, wait, 0)


def _gather_rows(x, tok_sorted, tile_nvalid, tr):
    R = tok_sorted.shape[0]
    D = x.shape[1]
    return pl.pallas_call(
        functools.partial(_gather_kernel, tr=tr),
        grid_spec=pltpu.PrefetchScalarGridSpec(
            num_scalar_prefetch=2,
            grid=(R // tr,),
            in_specs=[pl.BlockSpec(memory_space=pl.ANY)],
            out_specs=pl.BlockSpec((tr, D), lambda i, tok, nv: (i, 0)),
            scratch_shapes=[pltpu.SemaphoreType.DMA(())]),
        out_shape=jax.ShapeDtypeStruct((R, D), x.dtype),
        compiler_params=_cparams(1),
        name="moe_gather",
    )(tok_sorted, tile_nvalid, x)


def _experts_kernel(te_ref, nu_ref, x_ref, wg_ref, wu_ref, wd_ref, o_ref):
    i = pl.program_id(0)

    @pl.when(i < nu_ref[0])
    def _():
        x = x_ref[...].astype(BF16)
        hg = jnp.dot(x, wg_ref[0].astype(BF16), preferred_element_type=F32)
        hu = jnp.dot(x, wu_ref[0].astype(BF16), preferred_element_type=F32)
        h = hg * jax.nn.sigmoid(hg) * hu
        o_ref[...] = jnp.dot(h.astype(BF16), wd_ref[0].astype(BF16), preferred_element_type=F32)

    @pl.when(i >= nu_ref[0])
    def _():
        o_ref[...] = jnp.zeros(o_ref.shape, o_ref.dtype)


def _experts(xs, tile_expert, n_used, w_eg, w_eu, w_ed, tr):
    R, D = xs.shape
    E, _, FF = w_eg.shape
    used = lambda i, nu: jnp.minimum(i, nu[0] - 1)
    return pl.pallas_call(
        _experts_kernel,
        grid_spec=pltpu.PrefetchScalarGridSpec(
            num_scalar_prefetch=2,
            grid=(R // tr,),
            in_specs=[pl.BlockSpec((tr, D), lambda i, te, nu: (used(i, nu), 0)),
                      pl.BlockSpec((1, D, FF), lambda i, te, nu: (te[used(i, nu)], 0, 0)),
                      pl.BlockSpec((1, D, FF), lambda i, te, nu: (te[used(i, nu)], 0, 0)),
                      pl.BlockSpec((1, FF, D), lambda i, te, nu: (te[used(i, nu)], 0, 0))],
            out_specs=pl.BlockSpec((tr, D), lambda i, te, nu: (i, 0))),
        out_shape=jax.ShapeDtypeStruct((R, D), F32),
        compiler_params=_cparams(1),
        name="moe_experts",
    )(tile_expert, n_used, xs, w_eg, w_eu, w_ed)


def _combine_kernel(pos_ref, x1_ref, wt_ref, g_ref, ys_hbm, o_ref, buf, sem, *, tc):
    i = pl.program_id(0)

    def start(r, c):
        t = i * tc + r
        _row_copy(ys_hbm, buf.at[0], pos_ref[2 * t], r, sem).start()
        _row_copy(ys_hbm, buf.at[1], pos_ref[2 * t + 1], r, sem).start()
        return c

    def wait(r, c):
        _row_copy(ys_hbm, buf.at[0], 0, r, sem).wait()
        _row_copy(ys_hbm, buf.at[1], 0, r, sem).wait()
        return c

    lax.fori_loop(0, tc, start, 0)
    lax.fori_loop(0, tc, wait, 0)
    wt = wt_ref[...]
    y = x1_ref[...] + wt[:, 0:1] * buf[0] + wt[:, 1:2] * buf[1]
    o_ref[...] = y * lax.rsqrt(jnp.mean(y * y, axis=-1, keepdims=True) + RMS_EPS) * g_ref[...]


def _combine(x1, wts, g_final, ys, pos, tc=256):
    T, D = x1.shape
    tc = _tile(T, tc)
    return pl.pallas_call(
        functools.partial(_combine_kernel, tc=tc),
        grid_spec=pltpu.PrefetchScalarGridSpec(
            num_scalar_prefetch=1,
            grid=(T // tc,),
            in_specs=[pl.BlockSpec((tc, D), lambda i, pos: (i, 0)),
                      pl.BlockSpec((tc, LANES), lambda i, pos: (i, 0)),
                      pl.BlockSpec((1, D), lambda i, pos: (0, 0)),
                      pl.BlockSpec(memory_space=pl.ANY)],
            out_specs=pl.BlockSpec((tc, D), lambda i, pos: (i, 0)),
            scratch_shapes=[pltpu.VMEM((2, tc, D), F32), pltpu.SemaphoreType.DMA(())]),
        out_shape=jax.ShapeDtypeStruct((T, D), F32),
        compiler_params=_cparams(1),
        name="moe_combine",
    )(pos, x1, wts, g_final.reshape(1, D), ys)


def _sort_plan(eid, tr):
    T = eid.shape[0]
    n_pairs = 2 * T
    R = (-(-n_pairs // tr) + N_EXPERTS) * tr
    n_tiles = R // tr
    e = eid.reshape(n_pairs)
    onehot = (e[:, None] == jnp.arange(N_EXPERTS, dtype=jnp.int32)[None, :]).astype(jnp.int32)
    csum = jnp.cumsum(onehot, axis=0)
    counts = csum[-1]
    rank = jnp.sum(onehot * csum, axis=1) - 1
    padded = -(-counts // tr) * tr
    ends = jnp.cumsum(padded)
    starts = ends - padded
    pos = (starts[e] + rank).astype(jnp.int32)
    tok = jnp.arange(n_pairs, dtype=jnp.int32) // 2
    tok_sorted = jnp.zeros((R,), jnp.int32).at[pos].set(tok)
    tile_start = jnp.arange(n_tiles, dtype=jnp.int32) * tr
    tile_expert = jnp.minimum(jnp.searchsorted(ends, tile_start, side="right"),
                              N_EXPERTS - 1).astype(jnp.int32)
    tile_nvalid = jnp.clip(starts[tile_expert] + counts[tile_expert] - tile_start, 0, tr)
    n_used = (ends[-1] // tr).astype(jnp.int32).reshape(1)
    return pos, tok_sorted, tile_nvalid.astype(jnp.int32), tile_expert, n_used


def _pad_cols(w, n):
    return jnp.pad(w, ((0, 0), (0, n - w.shape[1])))


def _layer(x, Bp, Sp, Bs, Ss, state_pool, state_C, state_n, state_m, g_mix, w_in, b_if, w_pool,
           pool_scale, w_proj_a, w_proj_b, g_head, w_out, g_ffn, w_rg, b_rg, w_re, b_re,
           w_eg, w_eu, w_ed, g_out, moe_tile=256):
    T, D = x.shape
    Tp = Bp * Sp
    P = w_pool.shape[0] * w_pool.shape[1]
    nh = b_if.shape[0] // 2
    W = w_proj_b.shape[0]
    d = W // nh
    gate0 = P + 4 * W

    xn, gates = _rms_proj(x, g_mix, _pad_cols(w_in[:, gate0:gate0 + 2 * nh], LANES), BF16, tm=512)
    gates = gates[:, :2 * nh]
    u = _matmul(xn, w_in, 0, P, F32)
    k_scale = jnp.concatenate([jnp.ones((W,), F32), jnp.full((W,), d ** -0.5, F32),
                               jnp.ones((2 * W,), F32)])
    qkvo = _matmul(xn, w_in, P, 4 * W, BF16, scale=k_scale)
    gab = _matmul(xn, w_in[:, gate0 + 2 * nh:], 0, 2 * D, BF16)

    wp_bf = w_pool.astype(BF16)
    hist = jnp.pad(state_pool, ((0, 0), (POOL_HIST - state_pool.shape[1], 0), (0, 0)))
    a = jnp.concatenate([_pool_prompt(u, wp_bf, pool_scale, Bp, Sp),
                         _pool_sample(u, hist, wp_bf, pool_scale, Tp, Bs, Ss)], axis=0)
    nbuf = state_pool.shape[1]
    pool_p = u[:Tp].reshape(Bp, Sp, P)[:, Sp - nbuf:]
    pool_s = jnp.concatenate([state_pool, u[Tp:].reshape(Bs, Ss, P)], axis=1)[:, -nbuf:]

    gates_r = gates.T
    hb_p, C_p, n_p, m_p = _mlstm_prompt(qkvo, gates, gates_r, b_if, g_head, Bp, Sp, nh)
    bb = 2
    gr_s = gates_r[:, Tp:].reshape(2 * nh, Bs // bb, bb * Ss).transpose(1, 0, 2)
    hb_s, C_s, n_s, m_s = _mlstm_sample(qkvo, gates, gr_s, b_if, g_head, state_C, state_n, state_m,
                                        Tp, Bs, Ss, nh, bb=bb)
    hb = jnp.concatenate([hb_p, hb_s], axis=0)

    mix = _merge(a, hb, gab, w_proj_a, w_proj_b)
    x1 = _matmul(mix, w_out, 0, D, F32, res=x)

    w_r = _pad_cols(jnp.concatenate([w_rg, w_re], axis=1), LANES)
    b_r = _pad_cols(jnp.concatenate([b_rg, b_re]).reshape(1, -1), LANES)
    xn2, ids, wts = _router(x1, g_ffn, w_r, b_r)
    pos, tok_sorted, tile_nvalid, tile_expert, n_used = _sort_plan(ids[:, :2], moe_tile)
    xs = _gather_rows(xn2, tok_sorted, tile_nvalid, moe_tile)
    ys = _experts(xs, tile_expert, n_used, w_eg, w_eu, w_ed, moe_tile)
    y = _combine(x1, wts, g_out, ys, pos)
    states_p = (pool_p, C_p, n_p.reshape(Bp, nh, d), m_p.reshape(Bp, nh))
    states_s = (pool_s, C_s, n_s.reshape(Bs, nh, d), m_s.reshape(Bs, nh))
    return y, states_p, states_s


def kernel(x_prompt, x_sample, state_pool, state_C, state_n, state_m, g_mix, w_in, b_if, w_pool, pool_scale, w_proj_a, w_proj_b, g_head, w_out, g_ffn, w_router_group, b_router_group, w_router_expert, b_router_expert, w_exp_gate, w_exp_up, w_exp_down, g_final):
    Bp, Sp, D = x_prompt.shape
    Bs, Ss, _ = x_sample.shape
    depth = g_mix.shape[0]
    assert depth == 1, "the final norm is fused into the layer's last kernel"
    x = jnp.concatenate([x_prompt.reshape(Bp * Sp, D), x_sample.reshape(Bs * Ss, D)], axis=0)
    l = 0
    y, sp, ss = _layer(x, Bp, Sp, Bs, Ss, state_pool[l], state_C[l], state_n[l], state_m[l],
                       g_mix[l], w_in[l], b_if[l], w_pool[l], pool_scale[l], w_proj_a[l],
                       w_proj_b[l], g_head[l], w_out[l], g_ffn[l], w_router_group[l],
                       b_router_group[l], w_router_expert[l], b_router_expert[l], w_exp_gate[l],
                       w_exp_up[l], w_exp_down[l], g_final)
    Tp = Bp * Sp
    return (y[:Tp].reshape(Bp, Sp, D), y[Tp:].reshape(Bs, Ss, D),
            sp[0][None], sp[1][None], sp[2][None], sp[3][None],
            ss[0][None], ss[1][None], ss[2][None], ss[3][None])
```

```python
import functools
import math

import jax
import jax.numpy as jnp
from jax import lax
from jax.experimental import pallas as pl
from jax.experimental.pallas import tpu as pltpu

F32 = jnp.float32
BF16 = jnp.bfloat16
I32 = jnp.int32
HIGHEST = lax.Precision.HIGHEST

RMS_EPS = 1e-6
POOL_WINDOWS = (2, 4, 8, 16)
POOL_HIST = 16
SAMPLE_POS0 = 16384
N_EXPERT_GROUPS = 4
EXPERTS_PER_GROUP = 8
N_EXPERTS = N_EXPERT_GROUPS * EXPERTS_PER_GROUP
LANES = 128
SUBLANES = 8
VMEM_LIMIT = 56 * 1024 * 1024
MOE_TT = 256
MOE_TR = 256


def _cparams(n_axes):
    return pltpu.CompilerParams(dimension_semantics=("arbitrary",) * n_axes,
                                vmem_limit_bytes=VMEM_LIMIT)


def _tile(n, pref):
    t = min(n, pref)
    while n % t:
        t -= 1
    return t


def _split_specs(shape, n_first, axis_fn):
    first = pl.BlockSpec(shape, lambda *g: (jnp.minimum(axis_fn(*g), n_first - 1), 0))
    second = pl.BlockSpec(shape, lambda *g: (jnp.maximum(axis_fn(*g) - n_first, 0), 0))
    return first, second


def _rmsnorm(x, g):
    return x * lax.rsqrt(jnp.mean(x * x, axis=-1, keepdims=True) + RMS_EPS) * g


def _rms_proj_kernel(xp_ref, xs_ref, g_ref, w_ref, xn_ref, p_ref, *, n_first):
    def body(x):
        xn = _rmsnorm(x, g_ref[...])
        xn_ref[...] = xn.astype(xn_ref.dtype)
        p_ref[...] = jnp.dot(xn, w_ref[...], precision=HIGHEST, preferred_element_type=F32)

    i = pl.program_id(0)

    @pl.when(i < n_first)
    def _():
        body(xp_ref[...])

    @pl.when(i >= n_first)
    def _():
        body(xs_ref[...])


def _rms_proj(xp, xs, g, w_small, tm=512):
    Tp, D = xp.shape
    Ts = xs.shape[0]
    tm = _tile(math.gcd(Tp, Ts), tm)
    T = Tp + Ts
    sp, ss = _split_specs((tm, D), Tp // tm, lambda i: i)
    return pl.pallas_call(
        functools.partial(_rms_proj_kernel, n_first=Tp // tm),
        grid=(T // tm,),
        in_specs=[sp, ss,
                  pl.BlockSpec((1, D), lambda i: (0, 0)),
                  pl.BlockSpec((D, LANES), lambda i: (0, 0))],
        out_specs=[pl.BlockSpec((tm, D), lambda i: (i, 0)),
                   pl.BlockSpec((tm, LANES), lambda i: (i, 0))],
        out_shape=[jax.ShapeDtypeStruct((T, D), BF16),
                   jax.ShapeDtypeStruct((T, LANES), F32)],
        compiler_params=_cparams(1),
        name="rms_proj",
    )(xp, xs, g.reshape(1, D), w_small)


def _mm_kernel(*refs, has_scale, n_res_first, cast_w):
    x_ref, w_ref = refs[0], refs[1]
    k = 2
    scale_ref = None
    if has_scale:
        scale_ref = refs[k]; k += 1
    res_refs = ()
    if n_res_first is not None:
        res_refs = refs[k:k + 2]; k += 2
    o_ref = refs[k]
    if cast_w:
        wbf_ref = refs[k + 1]

        @pl.when(pl.program_id(1) == 0)
        def _():
            wbf_ref[...] = w_ref[...].astype(BF16)
    else:
        wbf_ref = w_ref

    acc = jnp.dot(x_ref[...], wbf_ref[...], preferred_element_type=F32)
    if has_scale:
        acc = acc * scale_ref[...]
    if n_res_first is None:
        o_ref[...] = acc.astype(o_ref.dtype)
    else:
        m = pl.program_id(1)

        @pl.when(m < n_res_first)
        def _():
            o_ref[...] = (acc + res_refs[0][...]).astype(o_ref.dtype)

        @pl.when(m >= n_res_first)
        def _():
            o_ref[...] = (acc + res_refs[1][...]).astype(o_ref.dtype)


def _matmul(x, w, col0, ncols, out_dtype, scale=None, res=None, tm=512, tn=1024):
    T, K = x.shape
    tn = _tile(math.gcd(ncols, col0) if col0 else ncols, tn)
    c0 = col0 // tn
    if res is not None:
        tm = _tile(math.gcd(res[0].shape[0], res[1].shape[0]), tm)
    tm = _tile(T, tm)
    cast_w = w.dtype != BF16
    in_specs = [pl.BlockSpec((tm, K), lambda n, m: (m, 0)),
                pl.BlockSpec((K, tn), lambda n, m: (0, c0 + n))]
    args = [x, w]
    if scale is not None:
        in_specs.append(pl.BlockSpec((1, tn), lambda n, m: (0, n)))
        args.append(scale.reshape(1, ncols))
    n_res_first = None
    if res is not None:
        n_res_first = res[0].shape[0] // tm
        first = pl.BlockSpec((tm, tn), lambda n, m: (jnp.minimum(m, n_res_first - 1), n))
        second = pl.BlockSpec((tm, tn), lambda n, m: (jnp.maximum(m - n_res_first, 0), n))
        in_specs += [first, second]
        args += list(res)
    return pl.pallas_call(
        functools.partial(_mm_kernel, has_scale=scale is not None, n_res_first=n_res_first,
                          cast_w=cast_w),
        grid=(ncols // tn, T // tm),
        in_specs=in_specs,
        out_specs=pl.BlockSpec((tm, tn), lambda n, m: (m, n)),
        out_shape=jax.ShapeDtypeStruct((T, ncols), out_dtype),
        scratch_shapes=[pltpu.VMEM((K, tn), BF16)] if cast_w else [],
        compiler_params=_cparams(2),
        name="mm_wstat",
    )(*args)


def _pool_prompt_kernel(u_ref, wp_ref, ps_ref, a_ref, ext_ref, *, ts, gc, pos0):
    s = pl.program_id(1)

    @pl.when(s == 0)
    def _():
        ext_ref[0:POOL_HIST, :] = jnp.zeros((POOL_HIST, ext_ref.shape[1]), F32)

    ext_ref[POOL_HIST:, :] = u_ref[...]
    t_abs = pos0 + s * ts + lax.broadcasted_iota(I32, (ts, 1), 0)
    for g, w in enumerate(POOL_WINDOWS):
        cols = slice(g * gc, (g + 1) * gc)
        e = ext_ref[:, cols]
        acc = e
        span = 1
        while span < w:
            acc = acc + pltpu.roll(acc, span, 0)
            span *= 2
        inv = 1.0 / jnp.minimum(t_abs + 1, w).astype(F32)
        pooled = acc[POOL_HIST:, :] * inv - e[POOL_HIST:, :]
        mixed = jnp.dot(pooled.astype(BF16), wp_ref[g], preferred_element_type=F32)
        a_ref[:, cols] = (mixed * ps_ref[:, cols]).astype(a_ref.dtype)
    ext_ref[0:POOL_HIST, :] = ext_ref[ts:ts + POOL_HIST, :]


def _pool_prompt(u_all, wp_bf, pool_scale, B, S, ts=512):
    P = u_all.shape[1]
    ts = _tile(S, ts)
    nst = S // ts
    G = len(POOL_WINDOWS)
    gc = P // G
    return pl.pallas_call(
        functools.partial(_pool_prompt_kernel, ts=ts, gc=gc, pos0=0),
        grid=(B, nst),
        in_specs=[pl.BlockSpec((ts, P), lambda b, s: (b * nst + s, 0)),
                  pl.BlockSpec((G, gc, gc), lambda b, s: (0, 0, 0)),
                  pl.BlockSpec((1, P), lambda b, s: (0, 0))],
        out_specs=pl.BlockSpec((ts, P), lambda b, s: (b * nst + s, 0)),
        out_shape=jax.ShapeDtypeStruct((B * S, P), BF16),
        scratch_shapes=[pltpu.VMEM((POOL_HIST + ts, P), F32)],
        compiler_params=_cparams(2),
        name="pool_prompt",
    )(u_all, wp_bf, pool_scale.reshape(1, P))


def _pool_sample_kernel(hist_ref, u_ref, wp_ref, ps_ref, a_ref, ext_ref, *, bb, sq, gc, pos0):
    ext_ref[:, 0:POOL_HIST, :] = hist_ref[...]
    ext_ref[:, POOL_HIST:, :] = u_ref[...]
    t_abs = pos0 + lax.broadcasted_iota(I32, (1, sq, 1), 1)
    for g, w in enumerate(POOL_WINDOWS):
        cols = slice(g * gc, (g + 1) * gc)
        cur = ext_ref[:, POOL_HIST:, cols]
        acc = cur
        for j in range(1, w):
            acc = acc + ext_ref[:, POOL_HIST - j:POOL_HIST - j + sq, cols]
        inv = 1.0 / jnp.minimum(t_abs + 1, w).astype(F32)
        pooled = (acc * inv - cur).reshape(bb * sq, gc)
        mixed = jnp.dot(pooled.astype(BF16), wp_ref[g], preferred_element_type=F32)
        a_ref[:, cols] = (mixed * ps_ref[:, cols]).astype(a_ref.dtype)


def _pool_sample(u_all, hist, wp_bf, pool_scale, row0, Bs, sq, bb=16):
    T, P = u_all.shape
    bb = _tile(Bs, bb)
    G = len(POOL_WINDOWS)
    gc = P // G
    u3 = u_all.reshape(T // sq, sq, P)
    blk0 = row0 // (sq * bb)
    return pl.pallas_call(
        functools.partial(_pool_sample_kernel, bb=bb, sq=sq, gc=gc, pos0=SAMPLE_POS0),
        grid=(Bs // bb,),
        in_specs=[pl.BlockSpec((bb, POOL_HIST, P), lambda i: (i, 0, 0)),
                  pl.BlockSpec((bb, sq, P), lambda i: (blk0 + i, 0, 0)),
                  pl.BlockSpec((G, gc, gc), lambda i: (0, 0, 0)),
                  pl.BlockSpec((1, P), lambda i: (0, 0))],
        out_specs=pl.BlockSpec((bb * sq, P), lambda i: (i, 0)),
        out_shape=jax.ShapeDtypeStruct((Bs * sq, P), BF16),
        scratch_shapes=[pltpu.VMEM((bb, POOL_HIST + sq, P), F32)],
        compiler_params=_cparams(1),
        name="pool_sample",
    )(hist, u3, wp_bf, pool_scale.reshape(1, P))


def _log_sigmoid(x):
    return jnp.minimum(x, 0.0) - jnp.log1p(jnp.exp(-jnp.abs(x)))


def _mlstm_cell(q, k, v, ig_c, ig_r, lf_c, lf_r, states, ls):
    L = q.shape[0]
    G = len(states)
    row = lax.broadcasted_iota(I32, (L, L), 0)
    col = lax.broadcasted_iota(I32, (L, L), 1)
    lower = col <= row
    upper = row <= col
    seg_c = None
    if G > 1:
        same = (row // ls) == (col // ls)
        lower = lower & same
        upper = upper & same
        seg_c = lax.broadcasted_iota(I32, (L, 1), 0) // ls

    def per_seq(vals):
        out = vals[0]
        for g in range(1, G):
            out = jnp.where(seg_c == g, vals[g], out)
        return out

    b_c = jnp.sum(jnp.where(lower, lf_r, 0.0), axis=1, keepdims=True)
    b_r = jnp.sum(jnp.where(upper, lf_c, 0.0), axis=0, keepdims=True)
    dmat = jnp.where(lower, b_c - b_r + ig_r, -jnp.inf)
    inter = b_c + per_seq([st[2] for st in states])
    m_new = jnp.maximum(inter, jnp.max(dmat, axis=1, keepdims=True))
    w_intra = jnp.exp(dmat - m_new)
    w_inter = jnp.exp(inter - m_new)
    s = lax.dot_general(q, k, (((1,), (1,)), ((), ())), preferred_element_type=F32) * w_intra
    qf = q.astype(F32)
    qc = [jnp.dot(q, st[0].astype(BF16), preferred_element_type=F32) for st in states]
    qn = [jnp.sum(qf * st[1], axis=1, keepdims=True) for st in states]
    if G > 1:
        q_c = qc[0]
        for g in range(1, G):
            q_c = jnp.where(seg_c == g, qc[g], q_c)
        q_n = per_seq(qn)
    else:
        q_c, q_n = qc[0], qn[0]
    num = jnp.dot(s.astype(BF16), v, preferred_element_type=F32) + w_inter * q_c
    den = jnp.sum(s, axis=1, keepdims=True) + w_inter * q_n
    h = num / jnp.maximum(jnp.abs(den), jnp.exp(-m_new))

    m_last = [m_new[(g + 1) * ls - 1:(g + 1) * ls, :] for g in range(G)]
    b_last = [b_c[(g + 1) * ls - 1:(g + 1) * ls, :] for g in range(G)]
    wl_c = jnp.exp(per_seq(b_last) - b_c + ig_c - per_seq(m_last))
    wk = k.astype(F32) * wl_c
    new_states = []
    for g, (C, n, m) in enumerate(states):
        wk_g = wk if G == 1 else jnp.where(seg_c == g, wk, 0.0)
        wl_inter = jnp.exp(b_last[g] + m - m_last[g])
        kv = lax.dot_general(wk_g.astype(BF16), v, (((0,), (0,)), ((), ())),
                             preferred_element_type=F32)
        new_states.append((wl_inter * C + kv,
                           wl_inter * n + jnp.sum(wk_g, axis=0, keepdims=True),
                           m_last[g]))
    return h, new_states


def _head_out(h, gh, o):
    return _rmsnorm(h, gh) * jax.nn.sigmoid(o.astype(F32))


def _gate_cols(g_c, g_r, bc_ref, br_ref):
    pre_c = g_c + br_ref[...]
    pre_r = g_r + bc_ref[...]
    return pre_c, _log_sigmoid(pre_c), pre_r, _log_sigmoid(pre_r)


def _mlstm_prompt_kernel(q_ref, k_ref, v_ref, o_ref, gc_ref, gr_ref, bc_ref, br_ref, gh_ref,
                         hb_ref, C_out, n_out, m_out, C_s, n_s, m_s, *, nh, d):
    c = pl.program_id(1)

    @pl.when(c == 0)
    def _():
        C_s[...] = jnp.zeros(C_s.shape, F32)
        n_s[...] = jnp.zeros(n_s.shape, F32)
        m_s[...] = jnp.zeros(m_s.shape, F32)

    L = q_ref.shape[0]
    pre_c, lf_call, pre_r, lf_rall = _gate_cols(gc_ref[...], gr_ref[...], bc_ref, br_ref)
    for hd in range(nh):
        cols = slice(hd * d, (hd + 1) * d)
        st = (C_s[hd], n_s[hd], m_s[hd])
        h, (new,) = _mlstm_cell(
            q_ref[:, cols], k_ref[:, cols], v_ref[:, cols],
            pre_c[:, hd:hd + 1], pre_r[hd:hd + 1, :],
            lf_call[:, nh + hd:nh + hd + 1], lf_rall[nh + hd:nh + hd + 1, :],
            [st], L)
        C_s[hd], n_s[hd], m_s[hd] = new
        hb_ref[:, cols] = _head_out(h, gh_ref[:, cols], o_ref[:, cols]).astype(hb_ref.dtype)

    @pl.when(c == pl.num_programs(1) - 1)
    def _():
        C_out[0] = C_s[...]
        n_out[0] = n_s[...]
        m_out[0] = m_s[...]


def _mlstm_prompt(qkvo, gates_c, gates_r, b_if, g_head, B, S, nh, chunk=256):
    W = qkvo.shape[1] // 4
    d = W // nh
    L = _tile(S, chunk)
    nc = S // L
    tok = lambda j: pl.BlockSpec((L, W), lambda b, c, j=j: (b * nc + c, j))
    const = lambda shape: pl.BlockSpec(shape, lambda b, c: (0,) * len(shape))
    return pl.pallas_call(
        functools.partial(_mlstm_prompt_kernel, nh=nh, d=d),
        grid=(B, nc),
        in_specs=[tok(0), tok(1), tok(2), tok(3),
                  pl.BlockSpec((L, 2 * nh), lambda b, c: (b * nc + c, 0)),
                  pl.BlockSpec((2 * nh, L), lambda b, c: (0, b * nc + c)),
                  const((2 * nh, 1)), const((1, 2 * nh)), const((1, W))],
        out_specs=[pl.BlockSpec((L, W), lambda b, c: (b * nc + c, 0)),
                   pl.BlockSpec((1, nh, d, d), lambda b, c: (b, 0, 0, 0)),
                   pl.BlockSpec((1, nh, 1, d), lambda b, c: (b, 0, 0, 0)),
                   pl.BlockSpec((1, nh, 1, 1), lambda b, c: (b, 0, 0, 0))],
        out_shape=[jax.ShapeDtypeStruct((B * S, W), BF16),
                   jax.ShapeDtypeStruct((B, nh, d, d), F32),
                   jax.ShapeDtypeStruct((B, nh, 1, d), F32),
                   jax.ShapeDtypeStruct((B, nh, 1, 1), F32)],
        scratch_shapes=[pltpu.VMEM((nh, d, d), F32), pltpu.VMEM((nh, 1, d), F32),
                        pltpu.VMEM((nh, 1, 1), F32)],
        compiler_params=_cparams(2),
        name="mlstm_prompt",
    )(qkvo, qkvo, qkvo, qkvo, gates_c, gates_r, b_if.reshape(2 * nh, 1), b_if.reshape(1, 2 * nh),
      g_head.reshape(1, W))


def _mlstm_sample_kernel(q_ref, k_ref, v_ref, o_ref, gc_ref, gr_ref, bc_ref, br_ref, gh_ref,
                         C_in, n_in, m_in, hb_ref, C_out, n_out, m_out, *, nh, d, bb, sq):
    pre_c, lf_call, pre_r, lf_rall = _gate_cols(gc_ref[...], gr_ref[0], bc_ref, br_ref)
    for hd in range(nh):
        cols = slice(hd * d, (hd + 1) * d)
        states = [(C_in[j, hd], n_in[j, hd], m_in[j, hd]) for j in range(bb)]
        h, new = _mlstm_cell(
            q_ref[:, cols], k_ref[:, cols], v_ref[:, cols],
            pre_c[:, hd:hd + 1], pre_r[hd:hd + 1, :],
            lf_call[:, nh + hd:nh + hd + 1], lf_rall[nh + hd:nh + hd + 1, :],
            states, sq)
        for j in range(bb):
            C_out[j, hd], n_out[j, hd], m_out[j, hd] = new[j]
        hb_ref[:, cols] = _head_out(h, gh_ref[:, cols], o_ref[:, cols]).astype(hb_ref.dtype)


def _mlstm_sample(qkvo, gates_c, gates_r, b_if, g_head, C0, n0, m0, row0, Bs, sq, nh, bb=2):
    W = qkvo.shape[1] // 4
    d = W // nh
    L = bb * sq
    blk0 = row0 // L
    tok = lambda j: pl.BlockSpec((L, W), lambda i, j=j: (blk0 + i, j))
    const = lambda shape: pl.BlockSpec(shape, lambda i: (0,) * len(shape))
    st = lambda a, b: pl.BlockSpec((bb, nh, a, b), lambda i: (i, 0, 0, 0))
    return pl.pallas_call(
        functools.partial(_mlstm_sample_kernel, nh=nh, d=d, bb=bb, sq=sq),
        grid=(Bs // bb,),
        in_specs=[tok(0), tok(1), tok(2), tok(3),
                  pl.BlockSpec((L, 2 * nh), lambda i: (blk0 + i, 0)),
                  pl.BlockSpec((1, 2 * nh, L), lambda i: (i, 0, 0)),
                  const((2 * nh, 1)), const((1, 2 * nh)), const((1, W)),
                  st(d, d), st(1, d), st(1, 1)],
        out_specs=[pl.BlockSpec((L, W), lambda i: (i, 0)), st(d, d), st(1, d), st(1, 1)],
        out_shape=[jax.ShapeDtypeStruct((Bs * sq, W), BF16),
                   jax.ShapeDtypeStruct((Bs, nh, d, d), F32),
                   jax.ShapeDtypeStruct((Bs, nh, 1, d), F32),
                   jax.ShapeDtypeStruct((Bs, nh, 1, 1), F32)],
        compiler_params=_cparams(1),
        name="mlstm_sample",
    )(qkvo, qkvo, qkvo, qkvo, gates_c, gates_r, b_if.reshape(2 * nh, 1), b_if.reshape(1, 2 * nh),
      g_head.reshape(1, W), C0, n0.reshape(Bs, nh, 1, d), m0.reshape(Bs, nh, 1, 1))


def _merge_kernel(ap_ref, as_ref, hp_ref, hs_ref, ga_ref, gb_ref, wa_ref, wb_ref, o_ref,
                  wa_bf, wb_bf, *, n_first):
    m = pl.program_id(1)

    @pl.when(m == 0)
    def _():
        wa_bf[...] = wa_ref[...].astype(BF16)
        wb_bf[...] = wb_ref[...].astype(BF16)

    def body(a, hb):
        pa = jnp.dot(a, wa_bf[...], preferred_element_type=F32)
        pb = jnp.dot(hb, wb_bf[...], preferred_element_type=F32)
        mix = (jax.nn.sigmoid(ga_ref[...].astype(F32)) * pa
               + jax.nn.sigmoid(gb_ref[...].astype(F32)) * pb)
        o_ref[...] = mix.astype(o_ref.dtype)

    @pl.when(m < n_first)
    def _():
        body(ap_ref[...], hp_ref[...])

    @pl.when(m >= n_first)
    def _():
        body(as_ref[...], hs_ref[...])


def _merge(a_p, a_s, hb_p, hb_s, gab, wa, wb, tm=512, tn=512):
    Tp, P = a_p.shape
    T = Tp + a_s.shape[0]
    W = hb_p.shape[1]
    D = wa.shape[1]
    tm = _tile(math.gcd(Tp, T - Tp), tm)
    tn = _tile(D, tn)
    nn = D // tn
    n_first = Tp // tm
    a_specs = _split_specs((tm, P), n_first, lambda n, m: m)
    h_specs = _split_specs((tm, W), n_first, lambda n, m: m)
    return pl.pallas_call(
        functools.partial(_merge_kernel, n_first=n_first),
        grid=(nn, T // tm),
        in_specs=[*a_specs, *h_specs,
                  pl.BlockSpec((tm, tn), lambda n, m: (m, n)),
                  pl.BlockSpec((tm, tn), lambda n, m: (m, nn + n)),
                  pl.BlockSpec((P, tn), lambda n, m: (0, n)),
                  pl.BlockSpec((W, tn), lambda n, m: (0, n))],
        out_specs=pl.BlockSpec((tm, tn), lambda n, m: (m, n)),
        out_shape=jax.ShapeDtypeStruct((T, D), BF16),
        scratch_shapes=[pltpu.VMEM((P, tn), BF16), pltpu.VMEM((W, tn), BF16)],
        compiler_params=_cparams(2),
        name="merge",
    )(a_p, a_s, hb_p, hb_s, gab, gab, wa, wb)


def _moe_local_rows(tt):
    return -(-(2 * tt + N_EXPERTS * (SUBLANES - 1)) // LANES) * LANES


def _router_kernel(x_ref, g_ref, w_ref, b_ref, xl_ref, meta_ref, cnt_ref, *, lr):
    tt = x_ref.shape[0]
    xn = _rmsnorm(x_ref[...], g_ref[...])
    logits = jnp.dot(xn, w_ref[...], precision=HIGHEST, preferred_element_type=F32) + b_ref[...]
    lane = lax.broadcasted_iota(I32, logits.shape, 1).astype(F32)
    neg = -jnp.inf
    gl = jnp.where(lane < N_EXPERT_GROUPS, logits, neg)
    gmax = jnp.max(gl, axis=1, keepdims=True)
    g_idx = jnp.min(jnp.where(gl == gmax, lane, float(LANES)), axis=1, keepdims=True)
    g_val = 1.0 / jnp.sum(jnp.exp(gl - gmax), axis=1, keepdims=True)
    lo = N_EXPERT_GROUPS + g_idx * EXPERTS_PER_GROUP
    el = jnp.where((lane >= lo) & (lane < lo + EXPERTS_PER_GROUP), logits, neg)
    t1 = jnp.max(el, axis=1, keepdims=True)
    i1 = jnp.min(jnp.where(el == t1, lane, float(LANES)), axis=1, keepdims=True)
    el2 = jnp.where(lane == i1, neg, el)
    t2 = jnp.max(el2, axis=1, keepdims=True)
    i2 = jnp.min(jnp.where(el2 == t2, lane, float(LANES)), axis=1, keepdims=True)
    e2 = jnp.exp(t2 - t1)
    w1 = g_val / (1.0 + e2)
    w2 = g_val * e2 / (1.0 + e2)

    oh1 = (lane == i1 - N_EXPERT_GROUPS).astype(F32)
    oh2 = (lane == i2 - N_EXPERT_GROUPS).astype(F32)
    r_i = lax.broadcasted_iota(I32, (tt, tt), 0)
    c_i = lax.broadcasted_iota(I32, (tt, tt), 1)
    before = (c_i < r_i).astype(BF16)
    rank1 = jnp.dot(before, oh1.astype(BF16), preferred_element_type=F32)
    rank2 = jnp.dot(before, oh2.astype(BF16), preferred_element_type=F32)
    cnt1 = jnp.sum(oh1, axis=0, keepdims=True)
    cnt = cnt1 + jnp.sum(oh2, axis=0, keepdims=True)
    cnt8 = jnp.floor((cnt + (SUBLANES - 1)) * (1.0 / SUBLANES)) * SUBLANES
    e_r = lax.broadcasted_iota(I32, (LANES, LANES), 0)
    e_c = lax.broadcasted_iota(I32, (LANES, LANES), 1)
    start = jnp.dot(jnp.broadcast_to(cnt8, (SUBLANES, LANES)), (e_r < e_c).astype(F32),
                    precision=HIGHEST, preferred_element_type=F32)[0:1]
    row1 = jnp.sum(oh1 * (start + rank1), axis=1, keepdims=True)
    row2 = jnp.sum(oh2 * (start + cnt1 + rank2), axis=1, keepdims=True)
    meta = jnp.where(lane == 0.0, row1, jnp.where(lane == 1.0, row2,
                     jnp.where(lane == 2.0, w1, jnp.where(lane == 3.0, w2, 0.0))))
    meta_ref[...] = meta
    cnt_ref[0] = cnt8.astype(I32)

    meta_t = meta.T
    dst = lax.broadcasted_iota(I32, (lr, tt), 0).astype(F32)
    place = ((dst == meta_t[0:1, :]) | (dst == meta_t[1:2, :])).astype(BF16)
    xl_ref[...] = jnp.dot(place, xn.astype(BF16), preferred_element_type=F32)


def _router(x1, g_ffn, w_r, b_r, tt):
    T, D = x1.shape
    lr = _moe_local_rows(tt)
    nj = T // tt
    return pl.pallas_call(
        functools.partial(_router_kernel, lr=lr),
        grid=(nj,),
        in_specs=[pl.BlockSpec((tt, D), lambda i: (i, 0)),
                  pl.BlockSpec((1, D), lambda i: (0, 0)),
                  pl.BlockSpec((D, LANES), lambda i: (0, 0)),
                  pl.BlockSpec((1, LANES), lambda i: (0, 0))],
        out_specs=[pl.BlockSpec((lr, D), lambda i: (i, 0)),
                   pl.BlockSpec((tt, LANES), lambda i: (i, 0)),
                   pl.BlockSpec((1, 1, LANES), lambda i: (i, 0, 0))],
        out_shape=[jax.ShapeDtypeStruct((nj * lr, D), F32),
                   jax.ShapeDtypeStruct((T, LANES), F32),
                   jax.ShapeDtypeStruct((nj, 1, LANES), I32)],
        compiler_params=_cparams(1),
        name="moe_route",
    )(x1, g_ffn.reshape(1, D), w_r, b_r)


def _moe_plan(cnt, tt, lr, tr):
    nj, E = cnt.shape
    n_tiles = nj * (2 * tt + E * (SUBLANES - 1)) // tr + E
    pre = jnp.cumsum(cnt, axis=0) - cnt
    tot = jnp.sum(cnt, axis=0)
    lstart = jnp.cumsum(cnt, axis=1) - cnt
    nt_e = (tot + tr - 1) // tr
    t_end = jnp.cumsum(nt_e)
    t_start = t_end - nt_e
    tile = jnp.arange(n_tiles, dtype=I32)
    tile_e = jnp.minimum(jnp.sum((t_end[None, :] <= tile[:, None]).astype(I32), axis=1), E - 1)
    tile_a = (tile - t_start[tile_e]) * tr
    n_used = t_end[-1].reshape(1)
    src_local = jnp.arange(nj, dtype=I32)[:, None] * lr + lstart
    src_global = t_start[None, :] * tr + pre
    as_i32 = lambda a: a.astype(I32)
    by_expert = lambda a: as_i32(a.T.reshape(-1))
    by_tile = lambda a: as_i32(a.reshape(-1))
    return dict(tile_e=as_i32(tile_e), tile_a=as_i32(tile_a), n_used=as_i32(n_used),
                pre_e=by_expert(pre), cnt_e=by_expert(cnt), src_e=by_expert(src_local),
                cnt_j=by_tile(cnt), dst_j=by_tile(lstart), src_j=by_tile(src_global),
                n_tiles=n_tiles)


def _experts_kernel(te_ref, ta_ref, nu_ref, pre_ref, cnt_ref, src_ref,
                    xl_hbm, wg_ref, wu_ref, wd_ref, o_ref,
                    xbuf, wg_bf, wu_bf, wd_bf, sem, *, nj, tr):
    i = pl.program_id(0)
    n_used = nu_ref[0]

    def for_each_run(tile, slot, fn):
        e = te_ref[tile]
        a = ta_ref[tile]

        def body(j, c):
            lo = pre_ref[e * nj + j]
            first = jnp.maximum(lo, a)
            n = jnp.minimum(lo + cnt_ref[e * nj + j], a + tr) - first

            @pl.when(n > 0)
            def _():
                src = pl.multiple_of(src_ref[e * nj + j] + (first - lo), SUBLANES)
                dst = pl.multiple_of(first - a, SUBLANES)
                rows = pl.multiple_of(n, SUBLANES)
                fn(pltpu.make_async_copy(xl_hbm.at[pl.ds(src, rows)],
                                         xbuf.at[slot, pl.ds(dst, rows)], sem.at[slot]))
            return c

        lax.fori_loop(0, nj, body, 0)

    @pl.when(i == 0)
    def _():
        xbuf[...] = jnp.zeros(xbuf.shape, xbuf.dtype)
        for_each_run(0, 0, lambda cp: cp.start())

    @pl.when(i + 1 < n_used)
    def _():
        for_each_run(i + 1, (i + 1) % 2, lambda cp: cp.start())

    @pl.when(i < n_used)
    def _():
        for_each_run(i, i % 2, lambda cp: cp.wait())

        @pl.when((i == 0) | (te_ref[i] != te_ref[jnp.maximum(i - 1, 0)]))
        def _():
            wg_bf[...] = wg_ref[0].astype(BF16)
            wu_bf[...] = wu_ref[0].astype(BF16)
            wd_bf[...] = wd_ref[0].astype(BF16)

        x = xbuf[i % 2].astype(BF16)
        hg = jnp.dot(x, wg_bf[...], preferred_element_type=F32)
        hu = jnp.dot(x, wu_bf[...], preferred_element_type=F32)
        h = hg * jax.nn.sigmoid(hg) * hu
        o_ref[...] = jnp.dot(h.astype(BF16), wd_bf[...], preferred_element_type=F32)

    @pl.when(i >= n_used)
    def _():
        o_ref[...] = jnp.zeros(o_ref.shape, o_ref.dtype)


def _experts(xl, plan, w_eg, w_eu, w_ed, nj, tr):
    D = xl.shape[1]
    E, _, FF = w_eg.shape
    n_tiles = plan["n_tiles"]
    used = lambda i, nu: jnp.minimum(i, nu[0] - 1)
    wspec = lambda shape: pl.BlockSpec(
        shape, lambda i, te, ta, nu, pre, cnt, src: (te[used(i, nu)], 0, 0))
    return pl.pallas_call(
        functools.partial(_experts_kernel, nj=nj, tr=tr),
        grid_spec=pltpu.PrefetchScalarGridSpec(
            num_scalar_prefetch=6,
            grid=(n_tiles,),
            in_specs=[pl.BlockSpec(memory_space=pl.ANY),
                      wspec((1, D, FF)), wspec((1, D, FF)), wspec((1, FF, D))],
            out_specs=pl.BlockSpec((tr, D), lambda i, *_: (i, 0)),
            scratch_shapes=[pltpu.VMEM((2, tr, D), F32),
                            pltpu.VMEM((D, FF), BF16), pltpu.VMEM((D, FF), BF16),
                            pltpu.VMEM((FF, D), BF16),
                            pltpu.SemaphoreType.DMA((2,))]),
        out_shape=jax.ShapeDtypeStruct((n_tiles * tr, D), F32),
        compiler_params=_cparams(1),
        name="moe_experts",
    )(plan["tile_e"], plan["tile_a"], plan["n_used"], plan["pre_e"], plan["cnt_e"], plan["src_e"],
      xl, w_eg, w_eu, w_ed)


def _combine_kernel(cnt_ref, dst_ref, src_ref, x1_ref, meta_ref, g_ref, ys_hbm, yp_ref, ys_ref,
                    ybuf, sem, *, n_first):
    j = pl.program_id(0)
    nj = pl.num_programs(0)
    lr = ybuf.shape[1]

    def for_each_run(tile, slot, fn):
        def body(e, c):
            n = cnt_ref[tile * N_EXPERTS + e]

            @pl.when(n > 0)
            def _():
                src = pl.multiple_of(src_ref[tile * N_EXPERTS + e], SUBLANES)
                dst = pl.multiple_of(dst_ref[tile * N_EXPERTS + e], SUBLANES)
                rows = pl.multiple_of(n, SUBLANES)
                fn(pltpu.make_async_copy(ys_hbm.at[pl.ds(src, rows)],
                                         ybuf.at[slot, pl.ds(dst, rows)], sem.at[slot]))
            return c

        lax.fori_loop(0, N_EXPERTS, body, 0)

    @pl.when(j == 0)
    def _():
        ybuf[...] = jnp.zeros(ybuf.shape, ybuf.dtype)
        for_each_run(0, 0, lambda cp: cp.start())

    @pl.when(j + 1 < nj)
    def _():
        for_each_run(j + 1, (j + 1) % 2, lambda cp: cp.start())

    for_each_run(j, j % 2, lambda cp: cp.wait())
    meta = meta_ref[...]
    col = lax.broadcasted_iota(I32, (meta.shape[0], lr), 1).astype(F32)
    pick = jnp.where(col == meta[:, 0:1], meta[:, 2:3],
                     jnp.where(col == meta[:, 1:2], meta[:, 3:4], 0.0))
    moe = jnp.dot(pick.astype(BF16), ybuf[j % 2].astype(BF16), preferred_element_type=F32)
    y = _rmsnorm(x1_ref[...] + moe, g_ref[...])

    @pl.when(j < n_first)
    def _():
        yp_ref[...] = y

    @pl.when(j >= n_first)
    def _():
        ys_ref[...] = y


def _combine(x1, meta, g_final, ys, plan, Tp, tt, lr):
    T, D = x1.shape
    nj = T // tt
    n_first = Tp // tt
    out_p, out_s = _split_specs((tt, D), n_first, lambda j, *_: j)
    return pl.pallas_call(
        functools.partial(_combine_kernel, n_first=n_first),
        grid_spec=pltpu.PrefetchScalarGridSpec(
            num_scalar_prefetch=3,
            grid=(nj,),
            in_specs=[pl.BlockSpec((tt, D), lambda j, *_: (j, 0)),
                      pl.BlockSpec((tt, LANES), lambda j, *_: (j, 0)),
                      pl.BlockSpec((1, D), lambda j, *_: (0, 0)),
                      pl.BlockSpec(memory_space=pl.ANY)],
            out_specs=[out_p, out_s],
            scratch_shapes=[pltpu.VMEM((2, lr, D), F32), pltpu.SemaphoreType.DMA((2,))]),
        out_shape=[jax.ShapeDtypeStruct((Tp, D), F32), jax.ShapeDtypeStruct((T - Tp, D), F32)],
        compiler_params=_cparams(1),
        name="moe_combine",
    )(plan["cnt_j"], plan["dst_j"], plan["src_j"], x1, meta, g_final.reshape(1, D), ys)


def _pad_cols(w, n):
    return jnp.pad(w, ((0, 0), (0, n - w.shape[1])))


def _layer(xp, xs, Bp, Sp, Bs, Ss, state_pool, state_C, state_n, state_m, g_mix, w_in, b_if,
           w_pool, pool_scale, w_proj_a, w_proj_b, g_head, w_out, g_ffn, w_rg, b_rg, w_re, b_re,
           w_eg, w_eu, w_ed, g_out):
    D = xp.shape[1]
    Tp = Bp * Sp
    T = Tp + Bs * Ss
    P = w_pool.shape[0] * w_pool.shape[1]
    nh = b_if.shape[0] // 2
    W = w_proj_b.shape[0]
    d = W // nh
    gate0 = P + 4 * W

    xn, gates = _rms_proj(xp, xs, g_mix, _pad_cols(w_in[:, gate0:gate0 + 2 * nh], LANES))
    gates = gates[:, :2 * nh]
    u = _matmul(xn, w_in, 0, P, F32)
    k_scale = jnp.concatenate([jnp.ones((W,), F32), jnp.full((W,), d ** -0.5, F32),
                               jnp.ones((2 * W,), F32)])
    qkvo = _matmul(xn, w_in, P, 4 * W, BF16, scale=k_scale)
    gab = _matmul(xn, w_in[:, gate0 + 2 * nh:].astype(BF16), 0, 2 * D, BF16)

    wp_bf = w_pool.astype(BF16)
    nbuf = state_pool.shape[1]
    hist = jnp.pad(state_pool, ((0, 0), (POOL_HIST - nbuf, 0), (0, 0)))
    a_p = _pool_prompt(u, wp_bf, pool_scale, Bp, Sp)
    a_s = _pool_sample(u, hist, wp_bf, pool_scale, Tp, Bs, Ss)
    pool_p = u[:Tp].reshape(Bp, Sp, P)[:, Sp - nbuf:]
    pool_s = jnp.concatenate([state_pool, u[Tp:].reshape(Bs, Ss, P)], axis=1)[:, -nbuf:]

    gates_r = gates.T
    hb_p, C_p, n_p, m_p = _mlstm_prompt(qkvo, gates, gates_r, b_if, g_head, Bp, Sp, nh)
    bb = 2
    gr_s = gates_r[:, Tp:].reshape(2 * nh, Bs // bb, bb * Ss).transpose(1, 0, 2)
    hb_s, C_s, n_s, m_s = _mlstm_sample(qkvo, gates, gr_s, b_if, g_head, state_C, state_n,
                                        state_m, Tp, Bs, Ss, nh, bb=bb)

    mix = _merge(a_p, a_s, hb_p, hb_s, gab, w_proj_a, w_proj_b)
    x1 = _matmul(mix, w_out, 0, D, F32, res=(xp, xs))

    tt = _tile(math.gcd(Tp, T - Tp), MOE_TT)
    lr = _moe_local_rows(tt)
    w_r = _pad_cols(jnp.concatenate([w_rg, w_re], axis=1), LANES)
    b_r = _pad_cols(jnp.concatenate([b_rg, b_re]).reshape(1, -1), LANES)
    xl, meta, cnt = _router(x1, g_ffn, w_r, b_r, tt)
    plan = _moe_plan(cnt[:, 0, :N_EXPERTS], tt, lr, MOE_TR)
    ys = _experts(xl, plan, w_eg, w_eu, w_ed, T // tt, MOE_TR)
    y_p, y_s = _combine(x1, meta, g_out, ys, plan, Tp, tt, lr)
    states_p = (pool_p, C_p, n_p.reshape(Bp, nh, d), m_p.reshape(Bp, nh))
    states_s = (pool_s, C_s, n_s.reshape(Bs, nh, d), m_s.reshape(Bs, nh))
    return y_p, y_s, states_p, states_s


def kernel(x_prompt, x_sample, state_pool, state_C, state_n, state_m, g_mix, w_in, b_if, w_pool, pool_scale, w_proj_a, w_proj_b, g_head, w_out, g_ffn, w_router_group, b_router_group, w_router_expert, b_router_expert, w_exp_gate, w_exp_up, w_exp_down, g_final):
    Bp, Sp, D = x_prompt.shape
    Bs, Ss, _ = x_sample.shape
    assert g_mix.shape[0] == 1, "the closing norm is fused into the layer's last kernel"
    l = 0
    y_p, y_s, sp, ss = _layer(
        x_prompt.reshape(Bp * Sp, D), x_sample.reshape(Bs * Ss, D), Bp, Sp, Bs, Ss,
        state_pool[l], state_C[l], state_n[l], state_m[l], g_mix[l], w_in[l], b_if[l], w_pool[l],
        pool_scale[l], w_proj_a[l], w_proj_b[l], g_head[l], w_out[l], g_ffn[l],
        w_router_group[l], b_router_group[l], w_router_expert[l], b_router_expert[l],
        w_exp_gate[l], w_exp_up[l], w_exp_down[l], g_final)
    return (y_p.reshape(Bp, Sp, D), y_s.reshape(Bs, Ss, D),
            sp[0][None], sp[1][None], sp[2][None], sp[3][None],
            ss[0][None], ss[1][None], ss[2][None], ss[3][None])
```

```python
import functools
import math

import jax
import jax.numpy as jnp
from jax import lax
from jax.experimental import pallas as pl
from jax.experimental.pallas import tpu as pltpu

F32 = jnp.float32
BF16 = jnp.bfloat16
I32 = jnp.int32

RMS_EPS = 1e-6
POOL_WINDOWS = (2, 4, 8, 16)
POOL_HIST = 16
SAMPLE_POS0 = 16384
N_EXPERT_GROUPS = 4
EXPERTS_PER_GROUP = 8
N_EXPERTS = N_EXPERT_GROUPS * EXPERTS_PER_GROUP
LANES = 128
SUBLANES = 8
VMEM_LIMIT = 56 * 1024 * 1024
MOE_TT = 256
MOE_TR = 256


def _cparams(n_axes):
    return pltpu.CompilerParams(dimension_semantics=("arbitrary",) * n_axes,
                                vmem_limit_bytes=VMEM_LIMIT)


def _tile(n, pref):
    t = min(n, pref)
    while n % t:
        t -= 1
    return t


def _split_specs(shape, n_first, axis_fn):
    first = pl.BlockSpec(shape, lambda *g: (jnp.minimum(axis_fn(*g), n_first - 1), 0))
    second = pl.BlockSpec(shape, lambda *g: (jnp.maximum(axis_fn(*g) - n_first, 0), 0))
    return first, second


def _rmsnorm(x, g):
    return x * lax.rsqrt(jnp.mean(x * x, axis=-1, keepdims=True) + RMS_EPS) * g


def _split_weight(w):
    w = _pad_cols(w, LANES)
    hi = w.astype(BF16)
    lo = (w - hi.astype(F32)).astype(BF16)
    return jnp.concatenate([hi, lo], axis=1)


def _dot_split(x, w_hl):
    xh = x.astype(BF16)
    xl = (x - xh.astype(F32)).astype(BF16)
    p = jnp.dot(xh, w_hl, preferred_element_type=F32)
    return (p[:, :LANES] + p[:, LANES:]
            + jnp.dot(xl, w_hl[:, :LANES], preferred_element_type=F32))


def _rms_proj_kernel(xp_ref, xs_ref, g_ref, w_ref, xn_ref, p_ref, *, n_first):
    def body(x):
        xn = _rmsnorm(x, g_ref[...])
        xn_ref[...] = xn.astype(xn_ref.dtype)
        p_ref[...] = _dot_split(xn, w_ref[...])

    i = pl.program_id(0)

    @pl.when(i < n_first)
    def _():
        body(xp_ref[...])

    @pl.when(i >= n_first)
    def _():
        body(xs_ref[...])


def _rms_proj(xp, xs, g, w_small, tm=512):
    Tp, D = xp.shape
    Ts = xs.shape[0]
    tm = _tile(math.gcd(Tp, Ts), tm)
    T = Tp + Ts
    sp, ss = _split_specs((tm, D), Tp // tm, lambda i: i)
    return pl.pallas_call(
        functools.partial(_rms_proj_kernel, n_first=Tp // tm),
        grid=(T // tm,),
        in_specs=[sp, ss,
                  pl.BlockSpec((1, D), lambda i: (0, 0)),
                  pl.BlockSpec((D, 2 * LANES), lambda i: (0, 0))],
        out_specs=[pl.BlockSpec((tm, D), lambda i: (i, 0)),
                   pl.BlockSpec((tm, LANES), lambda i: (i, 0))],
        out_shape=[jax.ShapeDtypeStruct((T, D), BF16),
                   jax.ShapeDtypeStruct((T, LANES), F32)],
        compiler_params=_cparams(1),
        name="rms_proj",
    )(xp, xs, g.reshape(1, D), _split_weight(w_small))


def _mm_kernel(*refs, n_res_first):
    x_ref, w_ref = refs[0], refs[1]
    res_refs = refs[2:4] if n_res_first is not None else ()
    o_ref, wbf_ref = refs[-2:]

    @pl.when(pl.program_id(1) == 0)
    def _():
        wbf_ref[...] = w_ref[...].astype(BF16)

    acc = jnp.dot(x_ref[...], wbf_ref[...], preferred_element_type=F32)
    if n_res_first is None:
        o_ref[...] = acc.astype(o_ref.dtype)
    else:
        m = pl.program_id(1)

        @pl.when(m < n_res_first)
        def _():
            o_ref[...] = (acc + res_refs[0][...]).astype(o_ref.dtype)

        @pl.when(m >= n_res_first)
        def _():
            o_ref[...] = (acc + res_refs[1][...]).astype(o_ref.dtype)


def _matmul(x, w, out_dtype, res=None, tm=512, tn=1024):
    T, K = x.shape
    ncols = w.shape[1]
    tn = _tile(ncols, tn)
    if res is not None:
        tm = _tile(math.gcd(res[0].shape[0], res[1].shape[0]), tm)
    tm = _tile(T, tm)
    in_specs = [pl.BlockSpec((tm, K), lambda n, m: (m, 0)),
                pl.BlockSpec((K, tn), lambda n, m: (0, n))]
    args = [x, w]
    n_res_first = None
    if res is not None:
        n_res_first = res[0].shape[0] // tm
        first = pl.BlockSpec((tm, tn), lambda n, m: (jnp.minimum(m, n_res_first - 1), n))
        second = pl.BlockSpec((tm, tn), lambda n, m: (jnp.maximum(m - n_res_first, 0), n))
        in_specs += [first, second]
        args += list(res)
    return pl.pallas_call(
        functools.partial(_mm_kernel, n_res_first=n_res_first),
        grid=(ncols // tn, T // tm),
        in_specs=in_specs,
        out_specs=pl.BlockSpec((tm, tn), lambda n, m: (m, n)),
        out_shape=jax.ShapeDtypeStruct((T, ncols), out_dtype),
        scratch_shapes=[pltpu.VMEM((K, tn), BF16)],
        compiler_params=_cparams(2),
        name="mm_wstat",
    )(*args)


def _mm_wt_kernel(*refs, has_scale, row0, tn):
    x_ref, wt_hbm = refs[0], refs[1]
    scale_ref = refs[2] if has_scale else None
    o_ref, stage, wbf_ref, sem = refs[-4:]
    n = pl.program_id(0)
    nn = pl.num_programs(0)

    def w_copy(tile, slot):
        start = pl.multiple_of(row0 + tile * tn, SUBLANES)
        return pltpu.make_async_copy(wt_hbm.at[pl.ds(start, tn)], stage.at[slot], sem.at[slot])

    @pl.when(pl.program_id(1) == 0)
    def _():
        @pl.when(n == 0)
        def _():
            w_copy(0, 0).start()

        w_copy(n, n % 2).wait()

        @pl.when(n + 1 < nn)
        def _():
            w_copy(n + 1, (n + 1) % 2).start()

        wbf_ref[...] = stage[n % 2].T.astype(BF16)

    acc = jnp.dot(x_ref[...], wbf_ref[...], preferred_element_type=F32)
    if has_scale:
        acc = acc * scale_ref[...]
    o_ref[...] = acc.astype(o_ref.dtype)


def _matmul_wt(x, w_t, row0, ncols, out_dtype, scale=None, tm=512, tn=1024):
    T, K = x.shape
    assert row0 % SUBLANES == 0
    tm = _tile(T, tm)
    tn = _tile(ncols, tn)
    in_specs = [pl.BlockSpec((tm, K), lambda n, m: (m, 0)),
                pl.BlockSpec(memory_space=pl.ANY)]
    args = [x, w_t]
    if scale is not None:
        in_specs.append(pl.BlockSpec((1, tn), lambda n, m: (0, n)))
        args.append(scale.reshape(1, ncols))
    return pl.pallas_call(
        functools.partial(_mm_wt_kernel, has_scale=scale is not None, row0=row0, tn=tn),
        grid=(ncols // tn, T // tm),
        in_specs=in_specs,
        out_specs=pl.BlockSpec((tm, tn), lambda n, m: (m, n)),
        out_shape=jax.ShapeDtypeStruct((T, ncols), out_dtype),
        scratch_shapes=[pltpu.VMEM((2, tn, K), F32), pltpu.VMEM((K, tn), BF16),
                        pltpu.SemaphoreType.DMA((2,))],
        compiler_params=_cparams(2),
        name="mm_wt",
    )(*args)


def _pool_prompt_kernel(u_ref, wp_ref, ps_ref, a_ref, ext_ref, *, ts, gc, pos0):
    s = pl.program_id(1)

    @pl.when(s == 0)
    def _():
        ext_ref[0:POOL_HIST, :] = jnp.zeros((POOL_HIST, ext_ref.shape[1]), F32)

    ext_ref[POOL_HIST:, :] = u_ref[...]
    t_abs = pos0 + s * ts + lax.broadcasted_iota(I32, (ts, 1), 0)
    for g, w in enumerate(POOL_WINDOWS):
        cols = slice(g * gc, (g + 1) * gc)
        e = ext_ref[:, cols]
        acc = e
        span = 1
        while span < w:
            acc = acc + pltpu.roll(acc, span, 0)
            span *= 2
        inv = 1.0 / jnp.minimum(t_abs + 1, w).astype(F32)
        pooled = acc[POOL_HIST:, :] * inv - e[POOL_HIST:, :]
        mixed = jnp.dot(pooled.astype(BF16), wp_ref[g], preferred_element_type=F32)
        a_ref[:, cols] = (mixed * ps_ref[:, cols]).astype(a_ref.dtype)
    ext_ref[0:POOL_HIST, :] = ext_ref[ts:ts + POOL_HIST, :]


def _pool_prompt(u_all, wp_bf, pool_scale, B, S, ts=512):
    P = u_all.shape[1]
    ts = _tile(S, ts)
    nst = S // ts
    G = len(POOL_WINDOWS)
    gc = P // G
    return pl.pallas_call(
        functools.partial(_pool_prompt_kernel, ts=ts, gc=gc, pos0=0),
        grid=(B, nst),
        in_specs=[pl.BlockSpec((ts, P), lambda b, s: (b * nst + s, 0)),
                  pl.BlockSpec((G, gc, gc), lambda b, s: (0, 0, 0)),
                  pl.BlockSpec((1, P), lambda b, s: (0, 0))],
        out_specs=pl.BlockSpec((ts, P), lambda b, s: (b * nst + s, 0)),
        out_shape=jax.ShapeDtypeStruct((B * S, P), BF16),
        scratch_shapes=[pltpu.VMEM((POOL_HIST + ts, P), F32)],
        compiler_params=_cparams(2),
        name="pool_prompt",
    )(u_all, wp_bf, pool_scale.reshape(1, P))


def _pool_sample_kernel(hist_ref, u_ref, wp_ref, ps_ref, a_ref, ext_ref, *, bb, sq, gc, pos0):
    ext_ref[:, 0:POOL_HIST, :] = hist_ref[...]
    ext_ref[:, POOL_HIST:, :] = u_ref[...]
    t_abs = pos0 + lax.broadcasted_iota(I32, (1, sq, 1), 1)
    for g, w in enumerate(POOL_WINDOWS):
        cols = slice(g * gc, (g + 1) * gc)
        cur = ext_ref[:, POOL_HIST:, cols]
        acc = cur
        for j in range(1, w):
            acc = acc + ext_ref[:, POOL_HIST - j:POOL_HIST - j + sq, cols]
        inv = 1.0 / jnp.minimum(t_abs + 1, w).astype(F32)
        pooled = (acc * inv - cur).reshape(bb * sq, gc)
        mixed = jnp.dot(pooled.astype(BF16), wp_ref[g], preferred_element_type=F32)
        a_ref[:, cols] = (mixed * ps_ref[:, cols]).astype(a_ref.dtype)


def _pool_sample(u_all, hist, wp_bf, pool_scale, row0, Bs, sq, bb=16):
    T, P = u_all.shape
    bb = _tile(Bs, bb)
    G = len(POOL_WINDOWS)
    gc = P // G
    u3 = u_all.reshape(T // sq, sq, P)
    blk0 = row0 // (sq * bb)
    return pl.pallas_call(
        functools.partial(_pool_sample_kernel, bb=bb, sq=sq, gc=gc, pos0=SAMPLE_POS0),
        grid=(Bs // bb,),
        in_specs=[pl.BlockSpec((bb, POOL_HIST, P), lambda i: (i, 0, 0)),
                  pl.BlockSpec((bb, sq, P), lambda i: (blk0 + i, 0, 0)),
                  pl.BlockSpec((G, gc, gc), lambda i: (0, 0, 0)),
                  pl.BlockSpec((1, P), lambda i: (0, 0))],
        out_specs=pl.BlockSpec((bb * sq, P), lambda i: (i, 0)),
        out_shape=jax.ShapeDtypeStruct((Bs * sq, P), BF16),
        scratch_shapes=[pltpu.VMEM((bb, POOL_HIST + sq, P), F32)],
        compiler_params=_cparams(1),
        name="pool_sample",
    )(hist, u3, wp_bf, pool_scale.reshape(1, P))


def _log_sigmoid(x):
    return jnp.minimum(x, 0.0) - jnp.log1p(jnp.exp(-jnp.abs(x)))


def _mlstm_cell(q, k, v, ig_c, ig_r, lf_c, lf_r, states, ls):
    L = q.shape[0]
    G = len(states)
    row = lax.broadcasted_iota(I32, (L, L), 0)
    col = lax.broadcasted_iota(I32, (L, L), 1)
    lower = col <= row
    upper = row <= col
    seg_c = None
    if G > 1:
        same = (row // ls) == (col // ls)
        lower = lower & same
        upper = upper & same
        seg_c = lax.broadcasted_iota(I32, (L, 1), 0) // ls

    def per_seq(vals):
        out = vals[0]
        for g in range(1, G):
            out = jnp.where(seg_c == g, vals[g], out)
        return out

    b_c = jnp.sum(jnp.where(lower, lf_r, 0.0), axis=1, keepdims=True)
    b_r = jnp.sum(jnp.where(upper, lf_c, 0.0), axis=0, keepdims=True)
    dmat = jnp.where(lower, b_c - b_r + ig_r, -jnp.inf)
    inter = b_c + per_seq([st[2] for st in states])
    m_new = jnp.maximum(inter, jnp.max(dmat, axis=1, keepdims=True))
    w_intra = jnp.exp(dmat - m_new)
    w_inter = jnp.exp(inter - m_new)
    s = lax.dot_general(q, k, (((1,), (1,)), ((), ())), preferred_element_type=F32) * w_intra
    qf = q.astype(F32)
    qc = [jnp.dot(q, st[0].astype(BF16), preferred_element_type=F32) for st in states]
    qn = [jnp.sum(qf * st[1], axis=1, keepdims=True) for st in states]
    if G > 1:
        q_c = qc[0]
        for g in range(1, G):
            q_c = jnp.where(seg_c == g, qc[g], q_c)
        q_n = per_seq(qn)
    else:
        q_c, q_n = qc[0], qn[0]
    num = jnp.dot(s.astype(BF16), v, preferred_element_type=F32) + w_inter * q_c
    den = jnp.sum(s, axis=1, keepdims=True) + w_inter * q_n
    h = num / jnp.maximum(jnp.abs(den), jnp.exp(-m_new))

    m_last = [m_new[(g + 1) * ls - 1:(g + 1) * ls, :] for g in range(G)]
    b_last = [b_c[(g + 1) * ls - 1:(g + 1) * ls, :] for g in range(G)]
    wl_c = jnp.exp(per_seq(b_last) - b_c + ig_c - per_seq(m_last))
    wk = k.astype(F32) * wl_c
    new_states = []
    for g, (C, n, m) in enumerate(states):
        wk_g = wk if G == 1 else jnp.where(seg_c == g, wk, 0.0)
        wl_inter = jnp.exp(b_last[g] + m - m_last[g])
        kv = lax.dot_general(wk_g.astype(BF16), v, (((0,), (0,)), ((), ())),
                             preferred_element_type=F32)
        new_states.append((wl_inter * C + kv,
                           wl_inter * n + jnp.sum(wk_g, axis=0, keepdims=True),
                           m_last[g]))
    return h, new_states


def _head_out(h, gh, o):
    return _rmsnorm(h, gh) * jax.nn.sigmoid(o.astype(F32))


def _gate_cols(g_c, g_r, bc_ref, br_ref):
    pre_c = g_c + br_ref[...]
    pre_r = g_r + bc_ref[...]
    return pre_c, _log_sigmoid(pre_c), pre_r, _log_sigmoid(pre_r)


def _mlstm_prompt_kernel(q_ref, k_ref, v_ref, o_ref, gc_ref, gr_ref, bc_ref, br_ref, gh_ref,
                         hb_ref, C_out, n_out, m_out, C_s, n_s, m_s, *, nh, d):
    c = pl.program_id(1)

    @pl.when(c == 0)
    def _():
        C_s[...] = jnp.zeros(C_s.shape, F32)
        n_s[...] = jnp.zeros(n_s.shape, F32)
        m_s[...] = jnp.zeros(m_s.shape, F32)

    L = q_ref.shape[0]
    pre_c, lf_call, pre_r, lf_rall = _gate_cols(gc_ref[...], gr_ref[...], bc_ref, br_ref)
    for hd in range(nh):
        cols = slice(hd * d, (hd + 1) * d)
        st = (C_s[hd], n_s[hd], m_s[hd])
        h, (new,) = _mlstm_cell(
            q_ref[:, cols], k_ref[:, cols], v_ref[:, cols],
            pre_c[:, hd:hd + 1], pre_r[hd:hd + 1, :],
            lf_call[:, nh + hd:nh + hd + 1], lf_rall[nh + hd:nh + hd + 1, :],
            [st], L)
        C_s[hd], n_s[hd], m_s[hd] = new
        hb_ref[:, cols] = _head_out(h, gh_ref[:, cols], o_ref[:, cols]).astype(hb_ref.dtype)

    @pl.when(c == pl.num_programs(1) - 1)
    def _():
        C_out[0] = C_s[...]
        n_out[0] = n_s[...]
        m_out[0] = m_s[...]


def _mlstm_prompt(qkvo, gates_c, gates_r, b_if, g_head, B, S, nh, chunk=256):
    W = qkvo.shape[1] // 4
    d = W // nh
    L = _tile(S, chunk)
    nc = S // L
    tok = lambda j: pl.BlockSpec((L, W), lambda b, c, j=j: (b * nc + c, j))
    const = lambda shape: pl.BlockSpec(shape, lambda b, c: (0,) * len(shape))
    return pl.pallas_call(
        functools.partial(_mlstm_prompt_kernel, nh=nh, d=d),
        grid=(B, nc),
        in_specs=[tok(0), tok(1), tok(2), tok(3),
                  pl.BlockSpec((L, 2 * nh), lambda b, c: (b * nc + c, 0)),
                  pl.BlockSpec((2 * nh, L), lambda b, c: (0, b * nc + c)),
                  const((2 * nh, 1)), const((1, 2 * nh)), const((1, W))],
        out_specs=[pl.BlockSpec((L, W), lambda b, c: (b * nc + c, 0)),
                   pl.BlockSpec((1, nh, d, d), lambda b, c: (b, 0, 0, 0)),
                   pl.BlockSpec((1, nh, 1, d), lambda b, c: (b, 0, 0, 0)),
                   pl.BlockSpec((1, nh, 1, 1), lambda b, c: (b, 0, 0, 0))],
        out_shape=[jax.ShapeDtypeStruct((B * S, W), BF16),
                   jax.ShapeDtypeStruct((B, nh, d, d), F32),
                   jax.ShapeDtypeStruct((B, nh, 1, d), F32),
                   jax.ShapeDtypeStruct((B, nh, 1, 1), F32)],
        scratch_shapes=[pltpu.VMEM((nh, d, d), F32), pltpu.VMEM((nh, 1, d), F32),
                        pltpu.VMEM((nh, 1, 1), F32)],
        compiler_params=_cparams(2),
        name="mlstm_prompt",
    )(qkvo, qkvo, qkvo, qkvo, gates_c, gates_r, b_if.reshape(2 * nh, 1), b_if.reshape(1, 2 * nh),
      g_head.reshape(1, W))


def _mlstm_sample_kernel(q_ref, k_ref, v_ref, o_ref, gc_ref, gr_ref, bc_ref, br_ref, gh_ref,
                         C_in, n_in, m_in, hb_ref, C_out, n_out, m_out, *, nh, d, bb, sq):
    pre_c, lf_call, pre_r, lf_rall = _gate_cols(gc_ref[...], gr_ref[0], bc_ref, br_ref)
    for hd in range(nh):
        cols = slice(hd * d, (hd + 1) * d)
        states = [(C_in[j, hd], n_in[j, hd], m_in[j, hd]) for j in range(bb)]
        h, new = _mlstm_cell(
            q_ref[:, cols], k_ref[:, cols], v_ref[:, cols],
            pre_c[:, hd:hd + 1], pre_r[hd:hd + 1, :],
            lf_call[:, nh + hd:nh + hd + 1], lf_rall[nh + hd:nh + hd + 1, :],
            states, sq)
        for j in range(bb):
            C_out[j, hd], n_out[j, hd], m_out[j, hd] = new[j]
        hb_ref[:, cols] = _head_out(h, gh_ref[:, cols], o_ref[:, cols]).astype(hb_ref.dtype)


def _mlstm_sample(qkvo, gates_c, gates_r, b_if, g_head, C0, n0, m0, row0, Bs, sq, nh, bb=2):
    W = qkvo.shape[1] // 4
    d = W // nh
    L = bb * sq
    blk0 = row0 // L
    tok = lambda j: pl.BlockSpec((L, W), lambda i, j=j: (blk0 + i, j))
    const = lambda shape: pl.BlockSpec(shape, lambda i: (0,) * len(shape))
    st = lambda a, b: pl.BlockSpec((bb, nh, a, b), lambda i: (i, 0, 0, 0))
    return pl.pallas_call(
        functools.partial(_mlstm_sample_kernel, nh=nh, d=d, bb=bb, sq=sq),
        grid=(Bs // bb,),
        in_specs=[tok(0), tok(1), tok(2), tok(3),
                  pl.BlockSpec((L, 2 * nh), lambda i: (blk0 + i, 0)),
                  pl.BlockSpec((1, 2 * nh, L), lambda i: (i, 0, 0)),
                  const((2 * nh, 1)), const((1, 2 * nh)), const((1, W)),
                  st(d, d), st(1, d), st(1, 1)],
        out_specs=[pl.BlockSpec((L, W), lambda i: (i, 0)), st(d, d), st(1, d), st(1, 1)],
        out_shape=[jax.ShapeDtypeStruct((Bs * sq, W), BF16),
                   jax.ShapeDtypeStruct((Bs, nh, d, d), F32),
                   jax.ShapeDtypeStruct((Bs, nh, 1, d), F32),
                   jax.ShapeDtypeStruct((Bs, nh, 1, 1), F32)],
        compiler_params=_cparams(1),
        name="mlstm_sample",
    )(qkvo, qkvo, qkvo, qkvo, gates_c, gates_r, b_if.reshape(2 * nh, 1), b_if.reshape(1, 2 * nh),
      g_head.reshape(1, W), C0, n0.reshape(Bs, nh, 1, d), m0.reshape(Bs, nh, 1, 1))


def _merge_kernel(ap_ref, as_ref, hp_ref, hs_ref, ga_ref, gb_ref, wa_ref, wb_ref, o_ref,
                  wa_bf, wb_bf, *, n_first):
    m = pl.program_id(1)

    @pl.when(m == 0)
    def _():
        wa_bf[...] = wa_ref[...].astype(BF16)
        wb_bf[...] = wb_ref[...].astype(BF16)

    def body(a, hb):
        pa = jnp.dot(a, wa_bf[...], preferred_element_type=F32)
        pb = jnp.dot(hb, wb_bf[...], preferred_element_type=F32)
        mix = (jax.nn.sigmoid(ga_ref[...].astype(F32)) * pa
               + jax.nn.sigmoid(gb_ref[...].astype(F32)) * pb)
        o_ref[...] = mix.astype(o_ref.dtype)

    @pl.when(m < n_first)
    def _():
        body(ap_ref[...], hp_ref[...])

    @pl.when(m >= n_first)
    def _():
        body(as_ref[...], hs_ref[...])


def _merge(a_p, a_s, hb_p, hb_s, gab, wa, wb, tm=512, tn=512):
    Tp, P = a_p.shape
    T = Tp + a_s.shape[0]
    W = hb_p.shape[1]
    D = wa.shape[1]
    tm = _tile(math.gcd(Tp, T - Tp), tm)
    tn = _tile(D, tn)
    nn = D // tn
    n_first = Tp // tm
    a_specs = _split_specs((tm, P), n_first, lambda n, m: m)
    h_specs = _split_specs((tm, W), n_first, lambda n, m: m)
    return pl.pallas_call(
        functools.partial(_merge_kernel, n_first=n_first),
        grid=(nn, T // tm),
        in_specs=[*a_specs, *h_specs,
                  pl.BlockSpec((tm, tn), lambda n, m: (m, n)),
                  pl.BlockSpec((tm, tn), lambda n, m: (m, nn + n)),
                  pl.BlockSpec((P, tn), lambda n, m: (0, n)),
                  pl.BlockSpec((W, tn), lambda n, m: (0, n))],
        out_specs=pl.BlockSpec((tm, tn), lambda n, m: (m, n)),
        out_shape=jax.ShapeDtypeStruct((T, D), BF16),
        scratch_shapes=[pltpu.VMEM((P, tn), BF16), pltpu.VMEM((W, tn), BF16)],
        compiler_params=_cparams(2),
        name="merge",
    )(a_p, a_s, hb_p, hb_s, gab, gab, wa, wb)


def _moe_local_rows(tt):
    return -(-(2 * tt + N_EXPERTS * (SUBLANES - 1)) // LANES) * LANES


def _router_kernel(x_ref, g_ref, w_ref, b_ref, xl_ref, meta_ref, cnt_ref, *, lr):
    tt = x_ref.shape[0]
    xn = _rmsnorm(x_ref[...], g_ref[...])
    logits = _dot_split(xn, w_ref[...]) + b_ref[...]
    lane = lax.broadcasted_iota(I32, logits.shape, 1).astype(F32)
    neg = -jnp.inf
    gl = jnp.where(lane < N_EXPERT_GROUPS, logits, neg)
    gmax = jnp.max(gl, axis=1, keepdims=True)
    g_idx = jnp.min(jnp.where(gl == gmax, lane, float(LANES)), axis=1, keepdims=True)
    g_val = 1.0 / jnp.sum(jnp.exp(gl - gmax), axis=1, keepdims=True)
    lo = N_EXPERT_GROUPS + g_idx * EXPERTS_PER_GROUP
    el = jnp.where((lane >= lo) & (lane < lo + EXPERTS_PER_GROUP), logits, neg)
    t1 = jnp.max(el, axis=1, keepdims=True)
    i1 = jnp.min(jnp.where(el == t1, lane, float(LANES)), axis=1, keepdims=True)
    el2 = jnp.where(lane == i1, neg, el)
    t2 = jnp.max(el2, axis=1, keepdims=True)
    i2 = jnp.min(jnp.where(el2 == t2, lane, float(LANES)), axis=1, keepdims=True)
    e2 = jnp.exp(t2 - t1)
    w1 = g_val / (1.0 + e2)
    w2 = g_val * e2 / (1.0 + e2)

    oh1 = (lane == i1 - N_EXPERT_GROUPS).astype(F32)
    oh2 = (lane == i2 - N_EXPERT_GROUPS).astype(F32)
    r_i = lax.broadcasted_iota(I32, (tt, tt), 0)
    c_i = lax.broadcasted_iota(I32, (tt, tt), 1)
    before = (c_i < r_i).astype(BF16)
    rank1 = jnp.dot(before, oh1.astype(BF16), preferred_element_type=F32)
    rank2 = jnp.dot(before, oh2.astype(BF16), preferred_element_type=F32)
    cnt1 = jnp.sum(oh1, axis=0, keepdims=True)
    cnt = cnt1 + jnp.sum(oh2, axis=0, keepdims=True)
    cnt8 = jnp.floor((cnt + (SUBLANES - 1)) * (1.0 / SUBLANES)) * SUBLANES
    e_r = lax.broadcasted_iota(I32, (LANES, LANES), 0)
    e_c = lax.broadcasted_iota(I32, (LANES, LANES), 1)
    start = jnp.dot(jnp.broadcast_to(cnt8, (2 * SUBLANES, LANES)).astype(BF16),
                    (e_r < e_c).astype(BF16), preferred_element_type=F32)[0:1]
    row1 = jnp.sum(oh1 * (start + rank1), axis=1, keepdims=True)
    row2 = jnp.sum(oh2 * (start + cnt1 + rank2), axis=1, keepdims=True)
    meta = jnp.where(lane == 0.0, row1, jnp.where(lane == 1.0, row2,
                     jnp.where(lane == 2.0, w1, jnp.where(lane == 3.0, w2, 0.0))))
    meta_ref[...] = meta
    cnt_ref[0] = cnt8.astype(I32)

    meta_t = meta.T
    dst = lax.broadcasted_iota(I32, (lr, tt), 0).astype(F32)
    place = ((dst == meta_t[0:1, :]) | (dst == meta_t[1:2, :])).astype(BF16)
    xl_ref[...] = jnp.dot(place, xn.astype(BF16), preferred_element_type=F32)


def _router(x1, g_ffn, w_r, b_r, tt):
    T, D = x1.shape
    lr = _moe_local_rows(tt)
    nj = T // tt
    return pl.pallas_call(
        functools.partial(_router_kernel, lr=lr),
        grid=(nj,),
        in_specs=[pl.BlockSpec((tt, D), lambda i: (i, 0)),
                  pl.BlockSpec((1, D), lambda i: (0, 0)),
                  pl.BlockSpec((D, 2 * LANES), lambda i: (0, 0)),
                  pl.BlockSpec((1, LANES), lambda i: (0, 0))],
        out_specs=[pl.BlockSpec((lr, D), lambda i: (i, 0)),
                   pl.BlockSpec((tt, LANES), lambda i: (i, 0)),
                   pl.BlockSpec((1, 1, LANES), lambda i: (i, 0, 0))],
        out_shape=[jax.ShapeDtypeStruct((nj * lr, D), F32),
                   jax.ShapeDtypeStruct((T, LANES), F32),
                   jax.ShapeDtypeStruct((nj, 1, LANES), I32)],
        compiler_params=_cparams(1),
        name="moe_route",
    )(x1, g_ffn.reshape(1, D), w_r, b_r)


def _moe_plan(cnt, tt, lr, tr):
    nj, E = cnt.shape
    n_tiles = nj * (2 * tt + E * (SUBLANES - 1)) // tr + E
    pre = jnp.cumsum(cnt, axis=0) - cnt
    tot = jnp.sum(cnt, axis=0)
    lstart = jnp.cumsum(cnt, axis=1) - cnt
    nt_e = (tot + tr - 1) // tr
    t_end = jnp.cumsum(nt_e)
    t_start = t_end - nt_e
    tile = jnp.arange(n_tiles, dtype=I32)
    tile_e = jnp.minimum(jnp.sum((t_end[None, :] <= tile[:, None]).astype(I32), axis=1), E - 1)
    tile_a = (tile - t_start[tile_e]) * tr
    tile_next = t_end[tile_e]
    n_used = t_end[-1].reshape(1)
    src_local = jnp.arange(nj, dtype=I32)[:, None] * lr + lstart
    src_global = t_start[None, :] * tr + pre
    as_i32 = lambda a: a.astype(I32)
    by_expert = lambda a: as_i32(a.T.reshape(-1))
    by_tile = lambda a: as_i32(a.reshape(-1))
    return dict(tile_e=as_i32(tile_e), tile_a=as_i32(tile_a), tile_next=as_i32(tile_next),
                n_used=as_i32(n_used),
                pre_e=by_expert(pre), cnt_e=by_expert(cnt), src_e=by_expert(src_local),
                cnt_j=by_tile(cnt), dst_j=by_tile(lstart), src_j=by_tile(src_global),
                n_tiles=n_tiles)


def _experts_kernel(te_ref, ta_ref, tn_ref, nu_ref, pre_ref, cnt_ref, src_ref,
                    xl_hbm, wg_hbm, wu_hbm, wd_hbm, o_ref,
                    xbuf, wg_st, wu_st, wd_st, wg_bf, wu_bf, wd_bf, sem, wsem, *, nj, tr):
    i = pl.program_id(0)
    n_used = nu_ref[0]

    def weight_copies(e):
        return (pltpu.make_async_copy(wg_hbm.at[e], wg_st, wsem.at[0]),
                pltpu.make_async_copy(wu_hbm.at[e], wu_st, wsem.at[1]),
                pltpu.make_async_copy(wd_hbm.at[e], wd_st, wsem.at[2]))

    def for_each_run(tile, slot, fn):
        e = te_ref[tile]
        a = ta_ref[tile]

        def body(j, c):
            lo = pre_ref[e * nj + j]
            first = jnp.maximum(lo, a)
            n = jnp.minimum(lo + cnt_ref[e * nj + j], a + tr) - first

            @pl.when(n > 0)
            def _():
                src = pl.multiple_of(src_ref[e * nj + j] + (first - lo), SUBLANES)
                dst = pl.multiple_of(first - a, SUBLANES)
                rows = pl.multiple_of(n, SUBLANES)
                fn(pltpu.make_async_copy(xl_hbm.at[pl.ds(src, rows)],
                                         xbuf.at[slot, pl.ds(dst, rows)], sem.at[slot]))
            return c

        lax.fori_loop(0, nj, body, 0)

    @pl.when((i == 0) & (n_used > 0))
    def _():
        xbuf[...] = jnp.zeros(xbuf.shape, xbuf.dtype)
        for_each_run(0, 0, lambda cp: cp.start())
        for cp in weight_copies(te_ref[0]):
            cp.start()

    @pl.when(i + 1 < n_used)
    def _():
        for_each_run(i + 1, (i + 1) % 2, lambda cp: cp.start())

    @pl.when(i < n_used)
    def _():
        @pl.when((i == 0) | (te_ref[i] != te_ref[jnp.maximum(i - 1, 0)]))
        def _():
            for cp in weight_copies(te_ref[i]):
                cp.wait()
            wg_bf[...] = wg_st[...].astype(BF16)
            wu_bf[...] = wu_st[...].astype(BF16)
            wd_bf[...] = wd_st[...].astype(BF16)
            nxt = tn_ref[i]

            @pl.when(nxt < n_used)
            def _():
                for cp in weight_copies(te_ref[nxt]):
                    cp.start()

        for_each_run(i, i % 2, lambda cp: cp.wait())
        x = xbuf[i % 2].astype(BF16)
        hg = jnp.dot(x, wg_bf[...], preferred_element_type=F32)
        hu = jnp.dot(x, wu_bf[...], preferred_element_type=F32)
        h = hg * jax.nn.sigmoid(hg) * hu
        o_ref[...] = jnp.dot(h.astype(BF16), wd_bf[...], preferred_element_type=F32)

    @pl.when(i >= n_used)
    def _():
        o_ref[...] = jnp.zeros(o_ref.shape, o_ref.dtype)


def _experts(xl, plan, w_eg, w_eu, w_ed, nj, tr):
    D = xl.shape[1]
    E, _, FF = w_eg.shape
    n_tiles = plan["n_tiles"]
    any_space = pl.BlockSpec(memory_space=pl.ANY)
    return pl.pallas_call(
        functools.partial(_experts_kernel, nj=nj, tr=tr),
        grid_spec=pltpu.PrefetchScalarGridSpec(
            num_scalar_prefetch=7,
            grid=(n_tiles,),
            in_specs=[any_space, any_space, any_space, any_space],
            out_specs=pl.BlockSpec((tr, D), lambda i, *_: (i, 0)),
            scratch_shapes=[pltpu.VMEM((2, tr, D), F32),
                            pltpu.VMEM((D, FF), F32), pltpu.VMEM((D, FF), F32),
                            pltpu.VMEM((FF, D), F32),
                            pltpu.VMEM((D, FF), BF16), pltpu.VMEM((D, FF), BF16),
                            pltpu.VMEM((FF, D), BF16),
                            pltpu.SemaphoreType.DMA((2,)), pltpu.SemaphoreType.DMA((3,))]),
        out_shape=jax.ShapeDtypeStruct((n_tiles * tr, D), F32),
        compiler_params=_cparams(1),
        name="moe_experts",
    )(plan["tile_e"], plan["tile_a"], plan["tile_next"], plan["n_used"], plan["pre_e"],
      plan["cnt_e"], plan["src_e"], xl, w_eg, w_eu, w_ed)


def _combine_kernel(cnt_ref, dst_ref, src_ref, x1_ref, meta_ref, g_ref, ys_hbm, yp_ref, ys_ref,
                    ybuf, sem, *, n_first):
    j = pl.program_id(0)
    nj = pl.num_programs(0)
    lr = ybuf.shape[1]

    def for_each_run(tile, slot, fn):
        def body(e, c):
            n = cnt_ref[tile * N_EXPERTS + e]

            @pl.when(n > 0)
            def _():
                src = pl.multiple_of(src_ref[tile * N_EXPERTS + e], SUBLANES)
                dst = pl.multiple_of(dst_ref[tile * N_EXPERTS + e], SUBLANES)
                rows = pl.multiple_of(n, SUBLANES)
                fn(pltpu.make_async_copy(ys_hbm.at[pl.ds(src, rows)],
                                         ybuf.at[slot, pl.ds(dst, rows)], sem.at[slot]))
            return c

        lax.fori_loop(0, N_EXPERTS, body, 0)

    @pl.when(j == 0)
    def _():
        ybuf[...] = jnp.zeros(ybuf.shape, ybuf.dtype)
        for_each_run(0, 0, lambda cp: cp.start())

    @pl.when(j + 1 < nj)
    def _():
        for_each_run(j + 1, (j + 1) % 2, lambda cp: cp.start())

    for_each_run(j, j % 2, lambda cp: cp.wait())
    meta = meta_ref[...]
    col = lax.broadcasted_iota(I32, (meta.shape[0], lr), 1).astype(F32)
    pick = jnp.where(col == meta[:, 0:1], meta[:, 2:3],
                     jnp.where(col == meta[:, 1:2], meta[:, 3:4], 0.0))
    moe = jnp.dot(pick.astype(BF16), ybuf[j % 2].astype(BF16), preferred_element_type=F32)
    y = _rmsnorm(x1_ref[...] + moe, g_ref[...])

    @pl.when(j < n_first)
    def _():
        yp_ref[...] = y

    @pl.when(j >= n_first)
    def _():
        ys_ref[...] = y


def _combine(x1, meta, g_final, ys, plan, Tp, tt, lr):
    T, D = x1.shape
    nj = T // tt
    n_first = Tp // tt
    out_p, out_s = _split_specs((tt, D), n_first, lambda j, *_: j)
    return pl.pallas_call(
        functools.partial(_combine_kernel, n_first=n_first),
        grid_spec=pltpu.PrefetchScalarGridSpec(
            num_scalar_prefetch=3,
            grid=(nj,),
            in_specs=[pl.BlockSpec((tt, D), lambda j, *_: (j, 0)),
                      pl.BlockSpec((tt, LANES), lambda j, *_: (j, 0)),
                      pl.BlockSpec((1, D), lambda j, *_: (0, 0)),
                      pl.BlockSpec(memory_space=pl.ANY)],
            out_specs=[out_p, out_s],
            scratch_shapes=[pltpu.VMEM((2, lr, D), F32), pltpu.SemaphoreType.DMA((2,))]),
        out_shape=[jax.ShapeDtypeStruct((Tp, D), F32), jax.ShapeDtypeStruct((T - Tp, D), F32)],
        compiler_params=_cparams(1),
        name="moe_combine",
    )(plan["cnt_j"], plan["dst_j"], plan["src_j"], x1, meta, g_final.reshape(1, D), ys)


def _pad_cols(w, n):
    return jnp.pad(w, ((0, 0), (0, n - w.shape[1])))


def _layer(xp, xs, Bp, Sp, Bs, Ss, state_pool, state_C, state_n, state_m, g_mix, w_in, b_if,
           w_pool, pool_scale, w_proj_a, w_proj_b, g_head, w_out, g_ffn, w_rg, b_rg, w_re, b_re,
           w_eg, w_eu, w_ed, g_out):
    D = xp.shape[1]
    Tp = Bp * Sp
    T = Tp + Bs * Ss
    P = w_pool.shape[0] * w_pool.shape[1]
    nh = b_if.shape[0] // 2
    W = w_proj_b.shape[0]
    d = W // nh
    gate0 = P + 4 * W

    w_in_t = w_in.T
    xn, gates = _rms_proj(xp, xs, g_mix, w_in_t[gate0:gate0 + 2 * nh].T)
    gates = gates[:, :2 * nh]
    u = _matmul_wt(xn, w_in_t, 0, P, F32)
    k_scale = jnp.concatenate([jnp.ones((W,), F32), jnp.full((W,), d ** -0.5, F32),
                               jnp.ones((2 * W,), F32)])
    qkvo = _matmul_wt(xn, w_in_t, P, 4 * W, BF16, scale=k_scale)
    gab = _matmul_wt(xn, w_in_t, gate0 + 2 * nh, 2 * D, BF16)

    wp_bf = w_pool.astype(BF16)
    nbuf = state_pool.shape[1]
    hist = jnp.pad(state_pool, ((0, 0), (POOL_HIST - nbuf, 0), (0, 0)))
    a_p = _pool_prompt(u, wp_bf, pool_scale, Bp, Sp)
    a_s = _pool_sample(u, hist, wp_bf, pool_scale, Tp, Bs, Ss)
    pool_p = u[:Tp].reshape(Bp, Sp, P)[:, Sp - nbuf:]
    pool_s = jnp.concatenate([state_pool, u[Tp:].reshape(Bs, Ss, P)], axis=1)[:, -nbuf:]

    gates_r = gates.T
    hb_p, C_p, n_p, m_p = _mlstm_prompt(qkvo, gates, gates_r, b_if, g_head, Bp, Sp, nh)
    bb = 2
    gr_s = gates_r[:, Tp:].reshape(2 * nh, Bs // bb, bb * Ss).transpose(1, 0, 2)
    hb_s, C_s, n_s, m_s = _mlstm_sample(qkvo, gates, gr_s, b_if, g_head, state_C, state_n,
                                        state_m, Tp, Bs, Ss, nh, bb=bb)

    mix = _merge(a_p, a_s, hb_p, hb_s, gab, w_proj_a, w_proj_b)
    x1 = _matmul(mix, w_out, F32, res=(xp, xs))

    tt = _tile(math.gcd(Tp, T - Tp), MOE_TT)
    lr = _moe_local_rows(tt)
    w_r = _split_weight(jnp.concatenate([w_rg, w_re], axis=1))
    b_r = _pad_cols(jnp.concatenate([b_rg, b_re]).reshape(1, -1), LANES)
    xl, meta, cnt = _router(x1, g_ffn, w_r, b_r, tt)
    plan = _moe_plan(cnt[:, 0, :N_EXPERTS], tt, lr, MOE_TR)
    ys = _experts(xl, plan, w_eg, w_eu, w_ed, T // tt, MOE_TR)
    y_p, y_s = _combine(x1, meta, g_out, ys, plan, Tp, tt, lr)
    states_p = (pool_p, C_p, n_p.reshape(Bp, nh, d), m_p.reshape(Bp, nh))
    states_s = (pool_s, C_s, n_s.reshape(Bs, nh, d), m_s.reshape(Bs, nh))
    return y_p, y_s, states_p, states_s


def kernel(x_prompt, x_sample, state_pool, state_C, state_n, state_m, g_mix, w_in, b_if, w_pool, pool_scale, w_proj_a, w_proj_b, g_head, w_out, g_ffn, w_router_group, b_router_group, w_router_expert, b_router_expert, w_exp_gate, w_exp_up, w_exp_down, g_final):
    Bp, Sp, D = x_prompt.shape
    Bs, Ss, _ = x_sample.shape
    assert g_mix.shape[0] == 1, "the closing norm is fused into the layer's last kernel"
    l = 0
    y_p, y_s, sp, ss = _layer(
        x_prompt.reshape(Bp * Sp, D), x_sample.reshape(Bs * Ss, D), Bp, Sp, Bs, Ss,
        state_pool[l], state_C[l], state_n[l], state_m[l], g_mix[l], w_in[l], b_if[l], w_pool[l],
        pool_scale[l], w_proj_a[l], w_proj_b[l], g_head[l], w_out[l], g_ffn[l],
        w_router_group[l], b_router_group[l], w_router_expert[l], b_router_expert[l],
        w_exp_gate[l], w_exp_up[l], w_exp_down[l], g_final)
    return (y_p.reshape(Bp, Sp, D), y_s.reshape(Bs, Ss, D),
            sp[0][None], sp[1][None], sp[2][None], sp[3][None],
            ss[0][None], ss[1][None], ss[2][None], ss[3][None])
```

```python
import functools
import math

import jax
import jax.numpy as jnp
from jax import lax
from jax.experimental import pallas as pl
from jax.experimental.pallas import tpu as pltpu

F32 = jnp.float32
BF16 = jnp.bfloat16
I32 = jnp.int32

RMS_EPS = 1e-6
POOL_WINDOWS = (2, 4, 8, 16)
POOL_HIST = 16
SAMPLE_POS0 = 16384
N_EXPERT_GROUPS = 4
EXPERTS_PER_GROUP = 8
N_EXPERTS = N_EXPERT_GROUPS * EXPERTS_PER_GROUP
LANES = 128
SUBLANES = 8
VMEM_LIMIT = 56 * 1024 * 1024
MOE_TT = 256
MOE_TR = 256


def _cparams(n_axes):
    return pltpu.CompilerParams(dimension_semantics=("arbitrary",) * n_axes,
                                vmem_limit_bytes=VMEM_LIMIT)


def _tile(n, pref):
    t = min(n, pref)
    while n % t:
        t -= 1
    return t


def _split_specs(shape, n_first, axis_fn):
    first = pl.BlockSpec(shape, lambda *g: (jnp.minimum(axis_fn(*g), n_first - 1), 0))
    second = pl.BlockSpec(shape, lambda *g: (jnp.maximum(axis_fn(*g) - n_first, 0), 0))
    return first, second


def _rmsnorm(x, g):
    return x * lax.rsqrt(jnp.mean(x * x, axis=-1, keepdims=True) + RMS_EPS) * g


def _split_weight(w):
    w = _pad_cols(w, LANES)
    hi = w.astype(BF16)
    lo = (w - hi.astype(F32)).astype(BF16)
    return jnp.concatenate([hi, lo], axis=1)


def _dot_split(x, w_hl):
    xh = x.astype(BF16)
    xl = (x - xh.astype(F32)).astype(BF16)
    p = jnp.dot(xh, w_hl, preferred_element_type=F32)
    return (p[:, :LANES] + p[:, LANES:]
            + jnp.dot(xl, w_hl[:, :LANES], preferred_element_type=F32))


def _rms_proj_kernel(xp_ref, xs_ref, g_ref, w_ref, xn_ref, p_ref, *, n_first):
    def body(x):
        xn = _rmsnorm(x, g_ref[...])
        xn_ref[...] = xn.astype(xn_ref.dtype)
        p_ref[...] = _dot_split(xn, w_ref[...])

    i = pl.program_id(0)

    @pl.when(i < n_first)
    def _():
        body(xp_ref[...])

    @pl.when(i >= n_first)
    def _():
        body(xs_ref[...])


def _rms_proj(xp, xs, g, w_small, tm=512):
    Tp, D = xp.shape
    Ts = xs.shape[0]
    tm = _tile(math.gcd(Tp, Ts), tm)
    T = Tp + Ts
    sp, ss = _split_specs((tm, D), Tp // tm, lambda i: i)
    return pl.pallas_call(
        functools.partial(_rms_proj_kernel, n_first=Tp // tm),
        grid=(T // tm,),
        in_specs=[sp, ss,
                  pl.BlockSpec((1, D), lambda i: (0, 0)),
                  pl.BlockSpec((D, 2 * LANES), lambda i: (0, 0))],
        out_specs=[pl.BlockSpec((tm, D), lambda i: (i, 0)),
                   pl.BlockSpec((tm, LANES), lambda i: (i, 0))],
        out_shape=[jax.ShapeDtypeStruct((T, D), BF16),
                   jax.ShapeDtypeStruct((T, LANES), F32)],
        compiler_params=_cparams(1),
        name="rms_proj",
    )(xp, xs, g.reshape(1, D), _split_weight(w_small))


def _mm_kernel(*refs, n_res_first):
    x_ref, w_ref = refs[0], refs[1]
    res_refs = refs[2:4] if n_res_first is not None else ()
    o_ref, wbf_ref = refs[-2:]

    @pl.when(pl.program_id(1) == 0)
    def _():
        wbf_ref[...] = w_ref[...].astype(BF16)

    acc = jnp.dot(x_ref[...], wbf_ref[...], preferred_element_type=F32)
    if n_res_first is None:
        o_ref[...] = acc.astype(o_ref.dtype)
    else:
        m = pl.program_id(1)

        @pl.when(m < n_res_first)
        def _():
            o_ref[...] = (acc + res_refs[0][...]).astype(o_ref.dtype)

        @pl.when(m >= n_res_first)
        def _():
            o_ref[...] = (acc + res_refs[1][...]).astype(o_ref.dtype)


def _matmul(x, w, out_dtype, res=None, tm=512, tn=1024):
    T, K = x.shape
    ncols = w.shape[1]
    tn = _tile(ncols, tn)
    if res is not None:
        tm = _tile(math.gcd(res[0].shape[0], res[1].shape[0]), tm)
    tm = _tile(T, tm)
    in_specs = [pl.BlockSpec((tm, K), lambda n, m: (m, 0)),
                pl.BlockSpec((K, tn), lambda n, m: (0, n))]
    args = [x, w]
    n_res_first = None
    if res is not None:
        n_res_first = res[0].shape[0] // tm
        first = pl.BlockSpec((tm, tn), lambda n, m: (jnp.minimum(m, n_res_first - 1), n))
        second = pl.BlockSpec((tm, tn), lambda n, m: (jnp.maximum(m - n_res_first, 0), n))
        in_specs += [first, second]
        args += list(res)
    return pl.pallas_call(
        functools.partial(_mm_kernel, n_res_first=n_res_first),
        grid=(ncols // tn, T // tm),
        in_specs=in_specs,
        out_specs=pl.BlockSpec((tm, tn), lambda n, m: (m, n)),
        out_shape=jax.ShapeDtypeStruct((T, ncols), out_dtype),
        scratch_shapes=[pltpu.VMEM((K, tn), BF16)],
        compiler_params=_cparams(2),
        name="mm_wstat",
    )(*args)


def _mm_wt_kernel(*refs, has_scale, row0, tn):
    x_ref, wt_hbm = refs[0], refs[1]
    scale_ref = refs[2] if has_scale else None
    o_ref, stage, wbf_ref, sem = refs[-4:]
    n = pl.program_id(0)
    nn = pl.num_programs(0)

    def w_copy(tile, slot):
        start = pl.multiple_of(row0 + tile * tn, SUBLANES)
        return pltpu.make_async_copy(wt_hbm.at[pl.ds(start, tn)], stage.at[slot], sem.at[slot])

    @pl.when(pl.program_id(1) == 0)
    def _():
        @pl.when(n == 0)
        def _():
            w_copy(0, 0).start()

        w_copy(n, n % 2).wait()

        @pl.when(n + 1 < nn)
        def _():
            w_copy(n + 1, (n + 1) % 2).start()

        wbf_ref[...] = stage[n % 2].T.astype(BF16)

    acc = jnp.dot(x_ref[...], wbf_ref[...], preferred_element_type=F32)
    if has_scale:
        acc = acc * scale_ref[...]
    o_ref[...] = acc.astype(o_ref.dtype)


def _matmul_wt(x, w_t, row0, ncols, out_dtype, scale=None, tm=1024, tn=1024):
    T, K = x.shape
    assert row0 % SUBLANES == 0
    tm = _tile(T, tm)
    tn = _tile(ncols, tn)
    in_specs = [pl.BlockSpec((tm, K), lambda n, m: (m, 0)),
                pl.BlockSpec(memory_space=pl.ANY)]
    args = [x, w_t]
    if scale is not None:
        in_specs.append(pl.BlockSpec((1, tn), lambda n, m: (0, n)))
        args.append(scale.reshape(1, ncols))
    return pl.pallas_call(
        functools.partial(_mm_wt_kernel, has_scale=scale is not None, row0=row0, tn=tn),
        grid=(ncols // tn, T // tm),
        in_specs=in_specs,
        out_specs=pl.BlockSpec((tm, tn), lambda n, m: (m, n)),
        out_shape=jax.ShapeDtypeStruct((T, ncols), out_dtype),
        scratch_shapes=[pltpu.VMEM((2, tn, K), F32), pltpu.VMEM((K, tn), BF16),
                        pltpu.SemaphoreType.DMA((2,))],
        compiler_params=_cparams(2),
        name="mm_wt",
    )(*args)


def _pool_prompt_kernel(u_ref, wp_ref, ps_ref, a_ref, ext_ref, *, ts, gc, pos0):
    s = pl.program_id(1)

    @pl.when(s == 0)
    def _():
        ext_ref[0:POOL_HIST, :] = jnp.zeros((POOL_HIST, ext_ref.shape[1]), F32)

    ext_ref[POOL_HIST:, :] = u_ref[...]
    t_abs = pos0 + s * ts + lax.broadcasted_iota(I32, (ts, 1), 0)
    for g, w in enumerate(POOL_WINDOWS):
        cols = slice(g * gc, (g + 1) * gc)
        e = ext_ref[:, cols]
        acc = e
        span = 1
        while span < w:
            acc = acc + pltpu.roll(acc, span, 0)
            span *= 2
        inv = 1.0 / jnp.minimum(t_abs + 1, w).astype(F32)
        pooled = acc[POOL_HIST:, :] * inv - e[POOL_HIST:, :]
        mixed = jnp.dot(pooled.astype(BF16), wp_ref[g], preferred_element_type=F32)
        a_ref[:, cols] = (mixed * ps_ref[:, cols]).astype(a_ref.dtype)
    ext_ref[0:POOL_HIST, :] = ext_ref[ts:ts + POOL_HIST, :]


def _pool_prompt(u_all, wp_bf, pool_scale, B, S, ts=512):
    P = u_all.shape[1]
    ts = _tile(S, ts)
    nst = S // ts
    G = len(POOL_WINDOWS)
    gc = P // G
    return pl.pallas_call(
        functools.partial(_pool_prompt_kernel, ts=ts, gc=gc, pos0=0),
        grid=(B, nst),
        in_specs=[pl.BlockSpec((ts, P), lambda b, s: (b * nst + s, 0)),
                  pl.BlockSpec((G, gc, gc), lambda b, s: (0, 0, 0)),
                  pl.BlockSpec((1, P), lambda b, s: (0, 0))],
        out_specs=pl.BlockSpec((ts, P), lambda b, s: (b * nst + s, 0)),
        out_shape=jax.ShapeDtypeStruct((B * S, P), BF16),
        scratch_shapes=[pltpu.VMEM((POOL_HIST + ts, P), F32)],
        compiler_params=_cparams(2),
        name="pool_prompt",
    )(u_all, wp_bf, pool_scale.reshape(1, P))


def _pool_sample_kernel(hist_ref, u_ref, wp_ref, ps_ref, a_ref, ext_ref, *, bb, sq, gc, pos0):
    ext_ref[:, 0:POOL_HIST, :] = hist_ref[...]
    ext_ref[:, POOL_HIST:, :] = u_ref[...]
    t_abs = pos0 + lax.broadcasted_iota(I32, (1, sq, 1), 1)
    for g, w in enumerate(POOL_WINDOWS):
        cols = slice(g * gc, (g + 1) * gc)
        cur = ext_ref[:, POOL_HIST:, cols]
        acc = cur
        for j in range(1, w):
            acc = acc + ext_ref[:, POOL_HIST - j:POOL_HIST - j + sq, cols]
        inv = 1.0 / jnp.minimum(t_abs + 1, w).astype(F32)
        pooled = (acc * inv - cur).reshape(bb * sq, gc)
        mixed = jnp.dot(pooled.astype(BF16), wp_ref[g], preferred_element_type=F32)
        a_ref[:, cols] = (mixed * ps_ref[:, cols]).astype(a_ref.dtype)


def _pool_sample(u_all, hist, wp_bf, pool_scale, row0, Bs, sq, bb=16):
    T, P = u_all.shape
    bb = _tile(Bs, bb)
    G = len(POOL_WINDOWS)
    gc = P // G
    u3 = u_all.reshape(T // sq, sq, P)
    blk0 = row0 // (sq * bb)
    return pl.pallas_call(
        functools.partial(_pool_sample_kernel, bb=bb, sq=sq, gc=gc, pos0=SAMPLE_POS0),
        grid=(Bs // bb,),
        in_specs=[pl.BlockSpec((bb, POOL_HIST, P), lambda i: (i, 0, 0)),
                  pl.BlockSpec((bb, sq, P), lambda i: (blk0 + i, 0, 0)),
                  pl.BlockSpec((G, gc, gc), lambda i: (0, 0, 0)),
                  pl.BlockSpec((1, P), lambda i: (0, 0))],
        out_specs=pl.BlockSpec((bb * sq, P), lambda i: (i, 0)),
        out_shape=jax.ShapeDtypeStruct((Bs * sq, P), BF16),
        scratch_shapes=[pltpu.VMEM((bb, POOL_HIST + sq, P), F32)],
        compiler_params=_cparams(1),
        name="pool_sample",
    )(hist, u3, wp_bf, pool_scale.reshape(1, P))


def _log_sigmoid(x):
    return jnp.minimum(x, 0.0) - jnp.log1p(jnp.exp(-jnp.abs(x)))


def _mlstm_cell(q, k, v, ig_c, ig_r, lf_c, lf_r, states, ls):
    L = q.shape[0]
    G = len(states)
    row = lax.broadcasted_iota(I32, (L, L), 0)
    col = lax.broadcasted_iota(I32, (L, L), 1)
    lower = col <= row
    upper = row <= col
    seg_c = None
    if G > 1:
        same = (row // ls) == (col // ls)
        lower = lower & same
        upper = upper & same
        seg_c = lax.broadcasted_iota(I32, (L, 1), 0) // ls

    def per_seq(vals):
        out = vals[0]
        for g in range(1, G):
            out = jnp.where(seg_c == g, vals[g], out)
        return out

    b_c = jnp.sum(jnp.where(lower, lf_r, 0.0), axis=1, keepdims=True)
    b_r = jnp.sum(jnp.where(upper, lf_c, 0.0), axis=0, keepdims=True)
    dmat = jnp.where(lower, b_c - b_r + ig_r, -jnp.inf)
    inter = b_c + per_seq([st[2] for st in states])
    m_new = jnp.maximum(inter, jnp.max(dmat, axis=1, keepdims=True))
    w_intra = jnp.exp(dmat - m_new)
    w_inter = jnp.exp(inter - m_new)
    s = lax.dot_general(q, k, (((1,), (1,)), ((), ())), preferred_element_type=F32) * w_intra
    qf = q.astype(F32)
    qc = [jnp.dot(q, st[0].astype(BF16), preferred_element_type=F32) for st in states]
    qn = [jnp.sum(qf * st[1], axis=1, keepdims=True) for st in states]
    if G > 1:
        q_c = qc[0]
        for g in range(1, G):
            q_c = jnp.where(seg_c == g, qc[g], q_c)
        q_n = per_seq(qn)
    else:
        q_c, q_n = qc[0], qn[0]
    num = jnp.dot(s.astype(BF16), v, preferred_element_type=F32) + w_inter * q_c
    den = jnp.sum(s, axis=1, keepdims=True) + w_inter * q_n
    h = num / jnp.maximum(jnp.abs(den), jnp.exp(-m_new))

    m_last = [m_new[(g + 1) * ls - 1:(g + 1) * ls, :] for g in range(G)]
    b_last = [b_c[(g + 1) * ls - 1:(g + 1) * ls, :] for g in range(G)]
    wl_c = jnp.exp(per_seq(b_last) - b_c + ig_c - per_seq(m_last))
    wk = k.astype(F32) * wl_c
    new_states = []
    for g, (C, n, m) in enumerate(states):
        wk_g = wk if G == 1 else jnp.where(seg_c == g, wk, 0.0)
        wl_inter = jnp.exp(b_last[g] + m - m_last[g])
        kv = lax.dot_general(wk_g.astype(BF16), v, (((0,), (0,)), ((), ())),
                             preferred_element_type=F32)
        new_states.append((wl_inter * C + kv,
                           wl_inter * n + jnp.sum(wk_g, axis=0, keepdims=True),
                           m_last[g]))
    return h, new_states


def _head_out(h, gh, o):
    return _rmsnorm(h, gh) * jax.nn.sigmoid(o.astype(F32))


def _gate_cols(g_c, g_r, bc_ref, br_ref):
    pre_c = g_c + br_ref[...]
    pre_r = g_r + bc_ref[...]
    return pre_c, _log_sigmoid(pre_c), pre_r, _log_sigmoid(pre_r)


def _mlstm_prompt_kernel(q_ref, k_ref, v_ref, o_ref, gc_ref, gr_ref, bc_ref, br_ref, gh_ref,
                         hb_ref, C_out, n_out, m_out, C_s, n_s, m_s, *, nh, d):
    c = pl.program_id(1)

    @pl.when(c == 0)
    def _():
        C_s[...] = jnp.zeros(C_s.shape, F32)
        n_s[...] = jnp.zeros(n_s.shape, F32)
        m_s[...] = jnp.zeros(m_s.shape, F32)

    L = q_ref.shape[0]
    pre_c, lf_call, pre_r, lf_rall = _gate_cols(gc_ref[...], gr_ref[...], bc_ref, br_ref)
    for hd in range(nh):
        cols = slice(hd * d, (hd + 1) * d)
        st = (C_s[hd], n_s[hd], m_s[hd])
        h, (new,) = _mlstm_cell(
            q_ref[:, cols], k_ref[:, cols], v_ref[:, cols],
            pre_c[:, hd:hd + 1], pre_r[hd:hd + 1, :],
            lf_call[:, nh + hd:nh + hd + 1], lf_rall[nh + hd:nh + hd + 1, :],
            [st], L)
        C_s[hd], n_s[hd], m_s[hd] = new
        hb_ref[:, cols] = _head_out(h, gh_ref[:, cols], o_ref[:, cols]).astype(hb_ref.dtype)

    @pl.when(c == pl.num_programs(1) - 1)
    def _():
        C_out[0] = C_s[...]
        n_out[0] = n_s[...]
        m_out[0] = m_s[...]


def _mlstm_prompt(qkvo, gates_c, gates_r, b_if, g_head, B, S, nh, chunk=256):
    W = qkvo.shape[1] // 4
    d = W // nh
    L = _tile(S, chunk)
    nc = S // L
    tok = lambda j: pl.BlockSpec((L, W), lambda b, c, j=j: (b * nc + c, j))
    const = lambda shape: pl.BlockSpec(shape, lambda b, c: (0,) * len(shape))
    return pl.pallas_call(
        functools.partial(_mlstm_prompt_kernel, nh=nh, d=d),
        grid=(B, nc),
        in_specs=[tok(0), tok(1), tok(2), tok(3),
                  pl.BlockSpec((L, 2 * nh), lambda b, c: (b * nc + c, 0)),
                  pl.BlockSpec((2 * nh, L), lambda b, c: (0, b * nc + c)),
                  const((2 * nh, 1)), const((1, 2 * nh)), const((1, W))],
        out_specs=[pl.BlockSpec((L, W), lambda b, c: (b * nc + c, 0)),
                   pl.BlockSpec((1, nh, d, d), lambda b, c: (b, 0, 0, 0)),
                   pl.BlockSpec((1, nh, 1, d), lambda b, c: (b, 0, 0, 0)),
                   pl.BlockSpec((1, nh, 1, 1), lambda b, c: (b, 0, 0, 0))],
        out_shape=[jax.ShapeDtypeStruct((B * S, W), BF16),
                   jax.ShapeDtypeStruct((B, nh, d, d), F32),
                   jax.ShapeDtypeStruct((B, nh, 1, d), F32),
                   jax.ShapeDtypeStruct((B, nh, 1, 1), F32)],
        scratch_shapes=[pltpu.VMEM((nh, d, d), F32), pltpu.VMEM((nh, 1, d), F32),
                        pltpu.VMEM((nh, 1, 1), F32)],
        compiler_params=_cparams(2),
        name="mlstm_prompt",
    )(qkvo, qkvo, qkvo, qkvo, gates_c, gates_r, b_if.reshape(2 * nh, 1), b_if.reshape(1, 2 * nh),
      g_head.reshape(1, W))


def _mlstm_sample_kernel(q_ref, k_ref, v_ref, o_ref, gc_ref, gr_ref, bc_ref, br_ref, gh_ref,
                         C_in, n_in, m_in, hb_ref, C_out, n_out, m_out, *, nh, d, bb, sq):
    pre_c, lf_call, pre_r, lf_rall = _gate_cols(gc_ref[...], gr_ref[0], bc_ref, br_ref)
    for hd in range(nh):
        cols = slice(hd * d, (hd + 1) * d)
        states = [(C_in[j, hd], n_in[j, hd], m_in[j, hd]) for j in range(bb)]
        h, new = _mlstm_cell(
            q_ref[:, cols], k_ref[:, cols], v_ref[:, cols],
            pre_c[:, hd:hd + 1], pre_r[hd:hd + 1, :],
            lf_call[:, nh + hd:nh + hd + 1], lf_rall[nh + hd:nh + hd + 1, :],
            states, sq)
        for j in range(bb):
            C_out[j, hd], n_out[j, hd], m_out[j, hd] = new[j]
        hb_ref[:, cols] = _head_out(h, gh_ref[:, cols], o_ref[:, cols]).astype(hb_ref.dtype)


def _mlstm_sample(qkvo, gates_c, gates_r, b_if, g_head, C0, n0, m0, row0, Bs, sq, nh, bb=2):
    W = qkvo.shape[1] // 4
    d = W // nh
    L = bb * sq
    blk0 = row0 // L
    tok = lambda j: pl.BlockSpec((L, W), lambda i, j=j: (blk0 + i, j))
    const = lambda shape: pl.BlockSpec(shape, lambda i: (0,) * len(shape))
    st = lambda a, b: pl.BlockSpec((bb, nh, a, b), lambda i: (i, 0, 0, 0))
    return pl.pallas_call(
        functools.partial(_mlstm_sample_kernel, nh=nh, d=d, bb=bb, sq=sq),
        grid=(Bs // bb,),
        in_specs=[tok(0), tok(1), tok(2), tok(3),
                  pl.BlockSpec((L, 2 * nh), lambda i: (blk0 + i, 0)),
                  pl.BlockSpec((1, 2 * nh, L), lambda i: (i, 0, 0)),
                  const((2 * nh, 1)), const((1, 2 * nh)), const((1, W)),
                  st(d, d), st(1, d), st(1, 1)],
        out_specs=[pl.BlockSpec((L, W), lambda i: (i, 0)), st(d, d), st(1, d), st(1, 1)],
        out_shape=[jax.ShapeDtypeStruct((Bs * sq, W), BF16),
                   jax.ShapeDtypeStruct((Bs, nh, d, d), F32),
                   jax.ShapeDtypeStruct((Bs, nh, 1, d), F32),
                   jax.ShapeDtypeStruct((Bs, nh, 1, 1), F32)],
        compiler_params=_cparams(1),
        name="mlstm_sample",
    )(qkvo, qkvo, qkvo, qkvo, gates_c, gates_r, b_if.reshape(2 * nh, 1), b_if.reshape(1, 2 * nh),
      g_head.reshape(1, W), C0, n0.reshape(Bs, nh, 1, d), m0.reshape(Bs, nh, 1, 1))


def _merge_kernel(ap_ref, as_ref, hp_ref, hs_ref, ga_ref, gb_ref, wa_ref, wb_ref, o_ref,
                  wa_bf, wb_bf, *, n_first):
    m = pl.program_id(1)

    @pl.when(m == 0)
    def _():
        wa_bf[...] = wa_ref[...].astype(BF16)
        wb_bf[...] = wb_ref[...].astype(BF16)

    def body(a, hb):
        pa = jnp.dot(a, wa_bf[...], preferred_element_type=F32)
        pb = jnp.dot(hb, wb_bf[...], preferred_element_type=F32)
        mix = (jax.nn.sigmoid(ga_ref[...].astype(F32)) * pa
               + jax.nn.sigmoid(gb_ref[...].astype(F32)) * pb)
        o_ref[...] = mix.astype(o_ref.dtype)

    @pl.when(m < n_first)
    def _():
        body(ap_ref[...], hp_ref[...])

    @pl.when(m >= n_first)
    def _():
        body(as_ref[...], hs_ref[...])


def _merge(a_p, a_s, hb_p, hb_s, gab, wa, wb, tm=512, tn=1024):
    Tp, P = a_p.shape
    T = Tp + a_s.shape[0]
    W = hb_p.shape[1]
    D = wa.shape[1]
    tm = _tile(math.gcd(Tp, T - Tp), tm)
    tn = _tile(D, tn)
    nn = D // tn
    n_first = Tp // tm
    a_specs = _split_specs((tm, P), n_first, lambda n, m: m)
    h_specs = _split_specs((tm, W), n_first, lambda n, m: m)
    return pl.pallas_call(
        functools.partial(_merge_kernel, n_first=n_first),
        grid=(nn, T // tm),
        in_specs=[*a_specs, *h_specs,
                  pl.BlockSpec((tm, tn), lambda n, m: (m, n)),
                  pl.BlockSpec((tm, tn), lambda n, m: (m, nn + n)),
                  pl.BlockSpec((P, tn), lambda n, m: (0, n)),
                  pl.BlockSpec((W, tn), lambda n, m: (0, n))],
        out_specs=pl.BlockSpec((tm, tn), lambda n, m: (m, n)),
        out_shape=jax.ShapeDtypeStruct((T, D), BF16),
        scratch_shapes=[pltpu.VMEM((P, tn), BF16), pltpu.VMEM((W, tn), BF16)],
        compiler_params=_cparams(2),
        name="merge",
    )(a_p, a_s, hb_p, hb_s, gab, gab, wa, wb)


def _moe_local_rows(tt):
    return -(-(2 * tt + N_EXPERTS * (SUBLANES - 1)) // LANES) * LANES


def _router_kernel(x_ref, g_ref, w_ref, b_ref, xl_ref, meta_ref, cnt_ref, *, lr):
    tt = x_ref.shape[0]
    xn = _rmsnorm(x_ref[...], g_ref[...])
    logits = _dot_split(xn, w_ref[...]) + b_ref[...]
    lane = lax.broadcasted_iota(I32, logits.shape, 1).astype(F32)
    neg = -jnp.inf
    gl = jnp.where(lane < N_EXPERT_GROUPS, logits, neg)
    gmax = jnp.max(gl, axis=1, keepdims=True)
    g_idx = jnp.min(jnp.where(gl == gmax, lane, float(LANES)), axis=1, keepdims=True)
    g_val = 1.0 / jnp.sum(jnp.exp(gl - gmax), axis=1, keepdims=True)
    lo = N_EXPERT_GROUPS + g_idx * EXPERTS_PER_GROUP
    el = jnp.where((lane >= lo) & (lane < lo + EXPERTS_PER_GROUP), logits, neg)
    t1 = jnp.max(el, axis=1, keepdims=True)
    i1 = jnp.min(jnp.where(el == t1, lane, float(LANES)), axis=1, keepdims=True)
    el2 = jnp.where(lane == i1, neg, el)
    t2 = jnp.max(el2, axis=1, keepdims=True)
    i2 = jnp.min(jnp.where(el2 == t2, lane, float(LANES)), axis=1, keepdims=True)
    e2 = jnp.exp(t2 - t1)
    w1 = g_val / (1.0 + e2)
    w2 = g_val * e2 / (1.0 + e2)

    oh1 = (lane == i1 - N_EXPERT_GROUPS).astype(F32)
    oh2 = (lane == i2 - N_EXPERT_GROUPS).astype(F32)
    r_i = lax.broadcasted_iota(I32, (tt, tt), 0)
    c_i = lax.broadcasted_iota(I32, (tt, tt), 1)
    before = (c_i < r_i).astype(BF16)
    rank1 = jnp.dot(before, oh1.astype(BF16), preferred_element_type=F32)
    rank2 = jnp.dot(before, oh2.astype(BF16), preferred_element_type=F32)
    cnt1 = jnp.sum(oh1, axis=0, keepdims=True)
    cnt = cnt1 + jnp.sum(oh2, axis=0, keepdims=True)
    cnt8 = jnp.floor((cnt + (SUBLANES - 1)) * (1.0 / SUBLANES)) * SUBLANES
    e_r = lax.broadcasted_iota(I32, (LANES, LANES), 0)
    e_c = lax.broadcasted_iota(I32, (LANES, LANES), 1)
    start = jnp.dot(jnp.broadcast_to(cnt8, (2 * SUBLANES, LANES)).astype(BF16),
                    (e_r < e_c).astype(BF16), preferred_element_type=F32)[0:1]
    row1 = jnp.sum(oh1 * (start + rank1), axis=1, keepdims=True)
    row2 = jnp.sum(oh2 * (start + cnt1 + rank2), axis=1, keepdims=True)
    meta = jnp.where(lane == 0.0, row1, jnp.where(lane == 1.0, row2,
                     jnp.where(lane == 2.0, w1, jnp.where(lane == 3.0, w2, 0.0))))
    meta_ref[...] = meta
    cnt_ref[0] = cnt8.astype(I32)

    meta_t = meta.T
    dst = lax.broadcasted_iota(I32, (lr, tt), 0).astype(F32)
    place = ((dst == meta_t[0:1, :]) | (dst == meta_t[1:2, :])).astype(BF16)
    xl_ref[...] = jnp.dot(place, xn.astype(BF16), preferred_element_type=F32)


def _router(x1, g_ffn, w_r, b_r, tt):
    T, D = x1.shape
    lr = _moe_local_rows(tt)
    nj = T // tt
    return pl.pallas_call(
        functools.partial(_router_kernel, lr=lr),
        grid=(nj,),
        in_specs=[pl.BlockSpec((tt, D), lambda i: (i, 0)),
                  pl.BlockSpec((1, D), lambda i: (0, 0)),
                  pl.BlockSpec((D, 2 * LANES), lambda i: (0, 0)),
                  pl.BlockSpec((1, LANES), lambda i: (0, 0))],
        out_specs=[pl.BlockSpec((lr, D), lambda i: (i, 0)),
                   pl.BlockSpec((tt, LANES), lambda i: (i, 0)),
                   pl.BlockSpec((1, 1, LANES), lambda i: (i, 0, 0))],
        out_shape=[jax.ShapeDtypeStruct((nj * lr, D), F32),
                   jax.ShapeDtypeStruct((T, LANES), F32),
                   jax.ShapeDtypeStruct((nj, 1, LANES), I32)],
        compiler_params=_cparams(1),
        name="moe_route",
    )(x1, g_ffn.reshape(1, D), w_r, b_r)


def _moe_plan(cnt, tt, lr, tr):
    nj, E = cnt.shape
    n_tiles = nj * (2 * tt + E * (SUBLANES - 1)) // tr + E
    pre = jnp.cumsum(cnt, axis=0) - cnt
    tot = jnp.sum(cnt, axis=0)
    lstart = jnp.cumsum(cnt, axis=1) - cnt
    nt_e = (tot + tr - 1) // tr
    t_end = jnp.cumsum(nt_e)
    t_start = t_end - nt_e
    tile = jnp.arange(n_tiles, dtype=I32)
    tile_e = jnp.minimum(jnp.sum((t_end[None, :] <= tile[:, None]).astype(I32), axis=1), E - 1)
    tile_a = (tile - t_start[tile_e]) * tr
    tile_next = t_end[tile_e]
    n_used = t_end[-1].reshape(1)
    src_local = jnp.arange(nj, dtype=I32)[:, None] * lr + lstart
    src_global = t_start[None, :] * tr + pre
    as_i32 = lambda a: a.astype(I32)
    by_expert = lambda a: as_i32(a.T.reshape(-1))
    by_tile = lambda a: as_i32(a.reshape(-1))
    return dict(tile_e=as_i32(tile_e), tile_a=as_i32(tile_a), tile_next=as_i32(tile_next),
                n_used=as_i32(n_used),
                pre_e=by_expert(pre), cnt_e=by_expert(cnt), src_e=by_expert(src_local),
                cnt_j=by_tile(cnt), dst_j=by_tile(lstart), src_j=by_tile(src_global),
                n_tiles=n_tiles)


def _experts_kernel(te_ref, ta_ref, tn_ref, nu_ref, pre_ref, cnt_ref, src_ref,
                    xl_hbm, wg_hbm, wu_hbm, wd_hbm, o_ref,
                    xbuf, wg_st, wu_st, wd_st, wg_bf, wu_bf, wd_bf, sem, wsem, *, nj, tr):
    i = pl.program_id(0)
    n_used = nu_ref[0]

    def weight_copies(e):
        return (pltpu.make_async_copy(wg_hbm.at[e], wg_st, wsem.at[0]),
                pltpu.make_async_copy(wu_hbm.at[e], wu_st, wsem.at[1]),
                pltpu.make_async_copy(wd_hbm.at[e], wd_st, wsem.at[2]))

    def for_each_run(tile, slot, fn):
        e = te_ref[tile]
        a = ta_ref[tile]

        def body(j, c):
            lo = pre_ref[e * nj + j]
            first = jnp.maximum(lo, a)
            n = jnp.minimum(lo + cnt_ref[e * nj + j], a + tr) - first

            @pl.when(n > 0)
            def _():
                src = pl.multiple_of(src_ref[e * nj + j] + (first - lo), SUBLANES)
                dst = pl.multiple_of(first - a, SUBLANES)
                rows = pl.multiple_of(n, SUBLANES)
                fn(pltpu.make_async_copy(xl_hbm.at[pl.ds(src, rows)],
                                         xbuf.at[slot, pl.ds(dst, rows)], sem.at[slot]))
            return c

        lax.fori_loop(0, nj, body, 0)

    @pl.when((i == 0) & (n_used > 0))
    def _():
        xbuf[...] = jnp.zeros(xbuf.shape, xbuf.dtype)
        for_each_run(0, 0, lambda cp: cp.start())
        for cp in weight_copies(te_ref[0]):
            cp.start()

    @pl.when(i + 1 < n_used)
    def _():
        for_each_run(i + 1, (i + 1) % 2, lambda cp: cp.start())

    @pl.when(i < n_used)
    def _():
        @pl.when((i == 0) | (te_ref[i] != te_ref[jnp.maximum(i - 1, 0)]))
        def _():
            for cp in weight_copies(te_ref[i]):
                cp.wait()
            wg_bf[...] = wg_st[...].astype(BF16)
            wu_bf[...] = wu_st[...].astype(BF16)
            wd_bf[...] = wd_st[...].astype(BF16)
            nxt = tn_ref[i]

            @pl.when(nxt < n_used)
            def _():
                for cp in weight_copies(te_ref[nxt]):
                    cp.start()

        for_each_run(i, i % 2, lambda cp: cp.wait())
        x = xbuf[i % 2].astype(BF16)
        hg = jnp.dot(x, wg_bf[...], preferred_element_type=F32)
        hu = jnp.dot(x, wu_bf[...], preferred_element_type=F32)
        h = hg * jax.nn.sigmoid(hg) * hu
        o_ref[...] = jnp.dot(h.astype(BF16), wd_bf[...], preferred_element_type=F32)

    @pl.when(i >= n_used)
    def _():
        o_ref[...] = jnp.zeros(o_ref.shape, o_ref.dtype)


def _experts(xl, plan, w_eg, w_eu, w_ed, nj, tr):
    D = xl.shape[1]
    E, _, FF = w_eg.shape
    n_tiles = plan["n_tiles"]
    any_space = pl.BlockSpec(memory_space=pl.ANY)
    return pl.pallas_call(
        functools.partial(_experts_kernel, nj=nj, tr=tr),
        grid_spec=pltpu.PrefetchScalarGridSpec(
            num_scalar_prefetch=7,
            grid=(n_tiles,),
            in_specs=[any_space, any_space, any_space, any_space],
            out_specs=pl.BlockSpec((tr, D), lambda i, *_: (i, 0)),
            scratch_shapes=[pltpu.VMEM((2, tr, D), F32),
                            pltpu.VMEM((D, FF), F32), pltpu.VMEM((D, FF), F32),
                            pltpu.VMEM((FF, D), F32),
                            pltpu.VMEM((D, FF), BF16), pltpu.VMEM((D, FF), BF16),
                            pltpu.VMEM((FF, D), BF16),
                            pltpu.SemaphoreType.DMA((2,)), pltpu.SemaphoreType.DMA((3,))]),
        out_shape=jax.ShapeDtypeStruct((n_tiles * tr, D), F32),
        compiler_params=_cparams(1),
        name="moe_experts",
    )(plan["tile_e"], plan["tile_a"], plan["tile_next"], plan["n_used"], plan["pre_e"],
      plan["cnt_e"], plan["src_e"], xl, w_eg, w_eu, w_ed)


def _combine_kernel(cnt_ref, dst_ref, src_ref, x1_ref, meta_ref, g_ref, ys_hbm, yp_ref, ys_ref,
                    ybuf, sem, *, n_first):
    j = pl.program_id(0)
    nj = pl.num_programs(0)
    lr = ybuf.shape[1]

    def for_each_run(tile, slot, fn):
        def body(e, c):
            n = cnt_ref[tile * N_EXPERTS + e]

            @pl.when(n > 0)
            def _():
                src = pl.multiple_of(src_ref[tile * N_EXPERTS + e], SUBLANES)
                dst = pl.multiple_of(dst_ref[tile * N_EXPERTS + e], SUBLANES)
                rows = pl.multiple_of(n, SUBLANES)
                fn(pltpu.make_async_copy(ys_hbm.at[pl.ds(src, rows)],
                                         ybuf.at[slot, pl.ds(dst, rows)], sem.at[slot]))
            return c

        lax.fori_loop(0, N_EXPERTS, body, 0)

    @pl.when(j == 0)
    def _():
        ybuf[...] = jnp.zeros(ybuf.shape, ybuf.dtype)
        for_each_run(0, 0, lambda cp: cp.start())

    @pl.when(j + 1 < nj)
    def _():
        for_each_run(j + 1, (j + 1) % 2, lambda cp: cp.start())

    for_each_run(j, j % 2, lambda cp: cp.wait())
    meta = meta_ref[...]
    col = lax.broadcasted_iota(I32, (meta.shape[0], lr), 1).astype(F32)
    pick = jnp.where(col == meta[:, 0:1], meta[:, 2:3],
                     jnp.where(col == meta[:, 1:2], meta[:, 3:4], 0.0))
    moe = jnp.dot(pick.astype(BF16), ybuf[j % 2].astype(BF16), preferred_element_type=F32)
    y = _rmsnorm(x1_ref[...] + moe, g_ref[...])

    @pl.when(j < n_first)
    def _():
        yp_ref[...] = y

    @pl.when(j >= n_first)
    def _():
        ys_ref[...] = y


def _combine(x1, meta, g_final, ys, plan, Tp, tt, lr):
    T, D = x1.shape
    nj = T // tt
    n_first = Tp // tt
    out_p, out_s = _split_specs((tt, D), n_first, lambda j, *_: j)
    return pl.pallas_call(
        functools.partial(_combine_kernel, n_first=n_first),
        grid_spec=pltpu.PrefetchScalarGridSpec(
            num_scalar_prefetch=3,
            grid=(nj,),
            in_specs=[pl.BlockSpec((tt, D), lambda j, *_: (j, 0)),
                      pl.BlockSpec((tt, LANES), lambda j, *_: (j, 0)),
                      pl.BlockSpec((1, D), lambda j, *_: (0, 0)),
                      pl.BlockSpec(memory_space=pl.ANY)],
            out_specs=[out_p, out_s],
            scratch_shapes=[pltpu.VMEM((2, lr, D), F32), pltpu.SemaphoreType.DMA((2,))]),
        out_shape=[jax.ShapeDtypeStruct((Tp, D), F32), jax.ShapeDtypeStruct((T - Tp, D), F32)],
        compiler_params=_cparams(1),
        name="moe_combine",
    )(plan["cnt_j"], plan["dst_j"], plan["src_j"], x1, meta, g_final.reshape(1, D), ys)


def _pad_cols(w, n):
    return jnp.pad(w, ((0, 0), (0, n - w.shape[1])))


def _layer(xp, xs, Bp, Sp, Bs, Ss, state_pool, state_C, state_n, state_m, g_mix, w_in, b_if,
           w_pool, pool_scale, w_proj_a, w_proj_b, g_head, w_out, g_ffn, w_rg, b_rg, w_re, b_re,
           w_eg, w_eu, w_ed, g_out):
    D = xp.shape[1]
    Tp = Bp * Sp
    T = Tp + Bs * Ss
    P = w_pool.shape[0] * w_pool.shape[1]
    nh = b_if.shape[0] // 2
    W = w_proj_b.shape[0]
    d = W // nh
    gate0 = P + 4 * W

    w_in_t = w_in.T
    xn, gates = _rms_proj(xp, xs, g_mix, w_in_t[gate0:gate0 + 2 * nh].T)
    gates = gates[:, :2 * nh]
    u = _matmul_wt(xn, w_in_t, 0, P, F32)
    k_scale = jnp.concatenate([jnp.ones((W,), F32), jnp.full((W,), d ** -0.5, F32),
                               jnp.ones((2 * W,), F32)])
    qkvo = _matmul_wt(xn, w_in_t, P, 4 * W, BF16, scale=k_scale)
    gab = _matmul_wt(xn, w_in_t, gate0 + 2 * nh, 2 * D, BF16)

    wp_bf = w_pool.astype(BF16)
    nbuf = state_pool.shape[1]
    hist = jnp.pad(state_pool, ((0, 0), (POOL_HIST - nbuf, 0), (0, 0)))
    a_p = _pool_prompt(u, wp_bf, pool_scale, Bp, Sp)
    a_s = _pool_sample(u, hist, wp_bf, pool_scale, Tp, Bs, Ss)
    pool_p = u[:Tp].reshape(Bp, Sp, P)[:, Sp - nbuf:]
    pool_s = jnp.concatenate([state_pool, u[Tp:].reshape(Bs, Ss, P)], axis=1)[:, -nbuf:]

    gates_r = gates.T
    hb_p, C_p, n_p, m_p = _mlstm_prompt(qkvo, gates, gates_r, b_if, g_head, Bp, Sp, nh)
    bb = 2
    gr_s = gates_r[:, Tp:].reshape(2 * nh, Bs // bb, bb * Ss).transpose(1, 0, 2)
    hb_s, C_s, n_s, m_s = _mlstm_sample(qkvo, gates, gr_s, b_if, g_head, state_C, state_n,
                                        state_m, Tp, Bs, Ss, nh, bb=bb)

    mix = _merge(a_p, a_s, hb_p, hb_s, gab, w_proj_a, w_proj_b)
    x1 = _matmul(mix, w_out, F32, res=(xp, xs))

    tt = _tile(math.gcd(Tp, T - Tp), MOE_TT)
    lr = _moe_local_rows(tt)
    w_r = _split_weight(jnp.concatenate([w_rg, w_re], axis=1))
    b_r = _pad_cols(jnp.concatenate([b_rg, b_re]).reshape(1, -1), LANES)
    xl, meta, cnt = _router(x1, g_ffn, w_r, b_r, tt)
    plan = _moe_plan(cnt[:, 0, :N_EXPERTS], tt, lr, MOE_TR)
    ys = _experts(xl, plan, w_eg, w_eu, w_ed, T // tt, MOE_TR)
    y_p, y_s = _combine(x1, meta, g_out, ys, plan, Tp, tt, lr)
    states_p = (pool_p, C_p, n_p.reshape(Bp, nh, d), m_p.reshape(Bp, nh))
    states_s = (pool_s, C_s, n_s.reshape(Bs, nh, d), m_s.reshape(Bs, nh))
    return y_p, y_s, states_p, states_s


def kernel(x_prompt, x_sample, state_pool, state_C, state_n, state_m, g_mix, w_in, b_if, w_pool, pool_scale, w_proj_a, w_proj_b, g_head, w_out, g_ffn, w_router_group, b_router_group, w_router_expert, b_router_expert, w_exp_gate, w_exp_up, w_exp_down, g_final):
    Bp, Sp, D = x_prompt.shape
    Bs, Ss, _ = x_sample.shape
    assert g_mix.shape[0] == 1, "the closing norm is fused into the layer's last kernel"
    l = 0
    y_p, y_s, sp, ss = _layer(
        x_prompt.reshape(Bp * Sp, D), x_sample.reshape(Bs * Ss, D), Bp, Sp, Bs, Ss,
        state_pool[l], state_C[l], state_n[l], state_m[l], g_mix[l], w_in[l], b_if[l], w_pool[l],
        pool_scale[l], w_proj_a[l], w_proj_b[l], g_head[l], w_out[l], g_ffn[l],
        w_router_group[l], b_router_group[l], w_router_expert[l], b_router_expert[l],
        w_exp_gate[l], w_exp_up[l], w_exp_down[l], g_final)
    return (y_p.reshape(Bp, Sp, D), y_s.reshape(Bs, Ss, D),
            sp[0][None], sp[1][None], sp[2][None], sp[3][None],
            ss[0][None], ss[1][None], ss[2][None], ss[3][None])
```

```python
import functools
import math

import jax
import jax.numpy as jnp
from jax import lax
from jax.experimental import pallas as pl
from jax.experimental.pallas import tpu as pltpu

F32 = jnp.float32
BF16 = jnp.bfloat16
I32 = jnp.int32

RMS_EPS = 1e-6
POOL_WINDOWS = (2, 4, 8, 16)
POOL_HIST = 16
SAMPLE_POS0 = 16384
N_EXPERT_GROUPS = 4
EXPERTS_PER_GROUP = 8
N_EXPERTS = N_EXPERT_GROUPS * EXPERTS_PER_GROUP
LANES = 128
SUBLANES = 8
VMEM_LIMIT = 56 * 1024 * 1024
MOE_TT = 256
MOE_TR = 256


def _cparams(n_axes):
    return pltpu.CompilerParams(dimension_semantics=("arbitrary",) * n_axes,
                                vmem_limit_bytes=VMEM_LIMIT)


def _tile(n, pref):
    t = min(n, pref)
    while n % t:
        t -= 1
    return t


def _split_specs(shape, n_first, axis_fn):
    first = pl.BlockSpec(shape, lambda *g: (jnp.minimum(axis_fn(*g), n_first - 1), 0))
    second = pl.BlockSpec(shape, lambda *g: (jnp.maximum(axis_fn(*g) - n_first, 0), 0))
    return first, second


def _rmsnorm(x, g):
    return x * lax.rsqrt(jnp.mean(x * x, axis=-1, keepdims=True) + RMS_EPS) * g


def _split_weight(w):
    w = _pad_cols(w, LANES)
    hi = w.astype(BF16)
    lo = (w - hi.astype(F32)).astype(BF16)
    return jnp.concatenate([hi, lo], axis=1)


def _dot_split(x, w_hl):
    xh = x.astype(BF16)
    xl = (x - xh.astype(F32)).astype(BF16)
    p = jnp.dot(xh, w_hl, preferred_element_type=F32)
    return (p[:, :LANES] + p[:, LANES:]
            + jnp.dot(xl, w_hl[:, :LANES], preferred_element_type=F32))


def _rms_proj_kernel(xp_ref, xs_ref, g_ref, w_ref, xn_ref, p_ref, *, n_first):
    def body(x):
        xn = _rmsnorm(x, g_ref[...])
        xn_ref[...] = xn.astype(xn_ref.dtype)
        p_ref[...] = _dot_split(xn, w_ref[...])

    i = pl.program_id(0)

    @pl.when(i < n_first)
    def _():
        body(xp_ref[...])

    @pl.when(i >= n_first)
    def _():
        body(xs_ref[...])


def _rms_proj(xp, xs, g, w_small, tm=512):
    Tp, D = xp.shape
    Ts = xs.shape[0]
    tm = _tile(math.gcd(Tp, Ts), tm)
    T = Tp + Ts
    sp, ss = _split_specs((tm, D), Tp // tm, lambda i: i)
    return pl.pallas_call(
        functools.partial(_rms_proj_kernel, n_first=Tp // tm),
        grid=(T // tm,),
        in_specs=[sp, ss,
                  pl.BlockSpec((1, D), lambda i: (0, 0)),
                  pl.BlockSpec((D, 2 * LANES), lambda i: (0, 0))],
        out_specs=[pl.BlockSpec((tm, D), lambda i: (i, 0)),
                   pl.BlockSpec((tm, LANES), lambda i: (i, 0))],
        out_shape=[jax.ShapeDtypeStruct((T, D), BF16),
                   jax.ShapeDtypeStruct((T, LANES), F32)],
        compiler_params=_cparams(1),
        name="rms_proj",
    )(xp, xs, g.reshape(1, D), _split_weight(w_small))


def _mm_kernel(*refs, n_res_first):
    x_ref, w_ref = refs[0], refs[1]
    res_refs = refs[2:4] if n_res_first is not None else ()
    o_ref, wbf_ref = refs[-2:]

    @pl.when(pl.program_id(1) == 0)
    def _():
        wbf_ref[...] = w_ref[...].astype(BF16)

    acc = jnp.dot(x_ref[...], wbf_ref[...], preferred_element_type=F32)
    if n_res_first is None:
        o_ref[...] = acc.astype(o_ref.dtype)
    else:
        m = pl.program_id(1)

        @pl.when(m < n_res_first)
        def _():
            o_ref[...] = (acc + res_refs[0][...]).astype(o_ref.dtype)

        @pl.when(m >= n_res_first)
        def _():
            o_ref[...] = (acc + res_refs[1][...]).astype(o_ref.dtype)


def _matmul(x, w, out_dtype, res=None, tm=512, tn=1024):
    T, K = x.shape
    ncols = w.shape[1]
    tn = _tile(ncols, tn)
    if res is not None:
        tm = _tile(math.gcd(res[0].shape[0], res[1].shape[0]), tm)
    tm = _tile(T, tm)
    in_specs = [pl.BlockSpec((tm, K), lambda n, m: (m, 0)),
                pl.BlockSpec((K, tn), lambda n, m: (0, n))]
    args = [x, w]
    n_res_first = None
    if res is not None:
        n_res_first = res[0].shape[0] // tm
        first = pl.BlockSpec((tm, tn), lambda n, m: (jnp.minimum(m, n_res_first - 1), n))
        second = pl.BlockSpec((tm, tn), lambda n, m: (jnp.maximum(m - n_res_first, 0), n))
        in_specs += [first, second]
        args += list(res)
    return pl.pallas_call(
        functools.partial(_mm_kernel, n_res_first=n_res_first),
        grid=(ncols // tn, T // tm),
        in_specs=in_specs,
        out_specs=pl.BlockSpec((tm, tn), lambda n, m: (m, n)),
        out_shape=jax.ShapeDtypeStruct((T, ncols), out_dtype),
        scratch_shapes=[pltpu.VMEM((K, tn), BF16)],
        compiler_params=_cparams(2),
        name="mm_wstat",
    )(*args)


def _mm_wt_kernel(*refs, has_scale, row0, tn):
    x_ref, wt_hbm = refs[0], refs[1]
    scale_ref = refs[2] if has_scale else None
    o_ref, stage, wbf_ref, sem = refs[-4:]
    n = pl.program_id(0)
    nn = pl.num_programs(0)

    def w_copy(tile, slot):
        start = pl.multiple_of(row0 + tile * tn, SUBLANES)
        return pltpu.make_async_copy(wt_hbm.at[pl.ds(start, tn)], stage.at[slot], sem.at[slot])

    @pl.when(pl.program_id(1) == 0)
    def _():
        @pl.when(n == 0)
        def _():
            w_copy(0, 0).start()

        w_copy(n, n % 2).wait()

        @pl.when(n + 1 < nn)
        def _():
            w_copy(n + 1, (n + 1) % 2).start()

        wbf_ref[...] = stage[n % 2].T.astype(BF16)

    acc = jnp.dot(x_ref[...], wbf_ref[...], preferred_element_type=F32)
    if has_scale:
        acc = acc * scale_ref[...]
    o_ref[...] = acc.astype(o_ref.dtype)


def _matmul_wt(x, w_t, row0, ncols, out_dtype, scale=None, tm=1024, tn=1024):
    T, K = x.shape
    assert row0 % SUBLANES == 0
    tm = _tile(T, tm)
    tn = _tile(ncols, tn)
    in_specs = [pl.BlockSpec((tm, K), lambda n, m: (m, 0)),
                pl.BlockSpec(memory_space=pl.ANY)]
    args = [x, w_t]
    if scale is not None:
        in_specs.append(pl.BlockSpec((1, tn), lambda n, m: (0, n)))
        args.append(scale.reshape(1, ncols))
    return pl.pallas_call(
        functools.partial(_mm_wt_kernel, has_scale=scale is not None, row0=row0, tn=tn),
        grid=(ncols // tn, T // tm),
        in_specs=in_specs,
        out_specs=pl.BlockSpec((tm, tn), lambda n, m: (m, n)),
        out_shape=jax.ShapeDtypeStruct((T, ncols), out_dtype),
        scratch_shapes=[pltpu.VMEM((2, tn, K), F32), pltpu.VMEM((K, tn), BF16),
                        pltpu.SemaphoreType.DMA((2,))],
        compiler_params=_cparams(2),
        name="mm_wt",
    )(*args)


def _pool_prompt_kernel(u_ref, wp_ref, ps_ref, a_ref, ext_ref, *, ts, gc, pos0):
    s = pl.program_id(1)

    @pl.when(s == 0)
    def _():
        ext_ref[0:POOL_HIST, :] = jnp.zeros((POOL_HIST, ext_ref.shape[1]), F32)

    ext_ref[POOL_HIST:, :] = u_ref[...]
    t_abs = pos0 + s * ts + lax.broadcasted_iota(I32, (ts, 1), 0)
    for g, w in enumerate(POOL_WINDOWS):
        cols = slice(g * gc, (g + 1) * gc)
        e = ext_ref[:, cols]
        acc = e
        span = 1
        while span < w:
            acc = acc + pltpu.roll(acc, span, 0)
            span *= 2
        inv = 1.0 / jnp.minimum(t_abs + 1, w).astype(F32)
        pooled = acc[POOL_HIST:, :] * inv - e[POOL_HIST:, :]
        mixed = jnp.dot(pooled.astype(BF16), wp_ref[g], preferred_element_type=F32)
        a_ref[:, cols] = (mixed * ps_ref[:, cols]).astype(a_ref.dtype)
    ext_ref[0:POOL_HIST, :] = ext_ref[ts:ts + POOL_HIST, :]


def _pool_prompt(u_all, wp_bf, pool_scale, B, S, ts=512):
    P = u_all.shape[1]
    ts = _tile(S, ts)
    nst = S // ts
    G = len(POOL_WINDOWS)
    gc = P // G
    return pl.pallas_call(
        functools.partial(_pool_prompt_kernel, ts=ts, gc=gc, pos0=0),
        grid=(B, nst),
        in_specs=[pl.BlockSpec((ts, P), lambda b, s: (b * nst + s, 0)),
                  pl.BlockSpec((G, gc, gc), lambda b, s: (0, 0, 0)),
                  pl.BlockSpec((1, P), lambda b, s: (0, 0))],
        out_specs=pl.BlockSpec((ts, P), lambda b, s: (b * nst + s, 0)),
        out_shape=jax.ShapeDtypeStruct((B * S, P), BF16),
        scratch_shapes=[pltpu.VMEM((POOL_HIST + ts, P), F32)],
        compiler_params=_cparams(2),
        name="pool_prompt",
    )(u_all, wp_bf, pool_scale.reshape(1, P))


def _pool_sample_kernel(hist_ref, u_ref, wp_ref, ps_ref, a_ref, ext_ref, *, bb, sq, gc, pos0):
    ext_ref[:, 0:POOL_HIST, :] = hist_ref[...]
    ext_ref[:, POOL_HIST:, :] = u_ref[...]
    t_abs = pos0 + lax.broadcasted_iota(I32, (1, sq, 1), 1)
    for g, w in enumerate(POOL_WINDOWS):
        cols = slice(g * gc, (g + 1) * gc)
        cur = ext_ref[:, POOL_HIST:, cols]
        acc = cur
        for j in range(1, w):
            acc = acc + ext_ref[:, POOL_HIST - j:POOL_HIST - j + sq, cols]
        inv = 1.0 / jnp.minimum(t_abs + 1, w).astype(F32)
        pooled = (acc * inv - cur).reshape(bb * sq, gc)
        mixed = jnp.dot(pooled.astype(BF16), wp_ref[g], preferred_element_type=F32)
        a_ref[:, cols] = (mixed * ps_ref[:, cols]).astype(a_ref.dtype)


def _pool_sample(u_all, hist, wp_bf, pool_scale, row0, Bs, sq, bb=16):
    T, P = u_all.shape
    bb = _tile(Bs, bb)
    G = len(POOL_WINDOWS)
    gc = P // G
    u3 = u_all.reshape(T // sq, sq, P)
    blk0 = row0 // (sq * bb)
    return pl.pallas_call(
        functools.partial(_pool_sample_kernel, bb=bb, sq=sq, gc=gc, pos0=SAMPLE_POS0),
        grid=(Bs // bb,),
        in_specs=[pl.BlockSpec((bb, POOL_HIST, P), lambda i: (i, 0, 0)),
                  pl.BlockSpec((bb, sq, P), lambda i: (blk0 + i, 0, 0)),
                  pl.BlockSpec((G, gc, gc), lambda i: (0, 0, 0)),
                  pl.BlockSpec((1, P), lambda i: (0, 0))],
        out_specs=pl.BlockSpec((bb * sq, P), lambda i: (i, 0)),
        out_shape=jax.ShapeDtypeStruct((Bs * sq, P), BF16),
        scratch_shapes=[pltpu.VMEM((bb, POOL_HIST + sq, P), F32)],
        compiler_params=_cparams(1),
        name="pool_sample",
    )(hist, u3, wp_bf, pool_scale.reshape(1, P))


def _log_sigmoid(x):
    return jnp.minimum(x, 0.0) - jnp.log1p(jnp.exp(-jnp.abs(x)))


def _mlstm_cell(q, k, v, ig_c, ig_r, lf_c, lf_r, states, ls):
    L = q.shape[0]
    G = len(states)
    row = lax.broadcasted_iota(I32, (L, L), 0)
    col = lax.broadcasted_iota(I32, (L, L), 1)
    lower = col <= row
    upper = row <= col
    seg_c = None
    if G > 1:
        same = (row // ls) == (col // ls)
        lower = lower & same
        upper = upper & same
        seg_c = lax.broadcasted_iota(I32, (L, 1), 0) // ls

    def per_seq(vals):
        out = vals[0]
        for g in range(1, G):
            out = jnp.where(seg_c == g, vals[g], out)
        return out

    b_c = jnp.sum(jnp.where(lower, lf_r, 0.0), axis=1, keepdims=True)
    b_r = jnp.sum(jnp.where(upper, lf_c, 0.0), axis=0, keepdims=True)
    dmat = jnp.where(lower, b_c - b_r + ig_r, -jnp.inf)
    inter = b_c + per_seq([st[2] for st in states])
    m_new = jnp.maximum(inter, jnp.max(dmat, axis=1, keepdims=True))
    w_intra = jnp.exp(dmat - m_new)
    w_inter = jnp.exp(inter - m_new)
    s = lax.dot_general(q, k, (((1,), (1,)), ((), ())), preferred_element_type=F32) * w_intra
    qf = q.astype(F32)
    qc = [jnp.dot(q, st[0].astype(BF16), preferred_element_type=F32) for st in states]
    qn = [jnp.sum(qf * st[1], axis=1, keepdims=True) for st in states]
    if G > 1:
        q_c = qc[0]
        for g in range(1, G):
            q_c = jnp.where(seg_c == g, qc[g], q_c)
        q_n = per_seq(qn)
    else:
        q_c, q_n = qc[0], qn[0]
    num = jnp.dot(s.astype(BF16), v, preferred_element_type=F32) + w_inter * q_c
    den = jnp.sum(s, axis=1, keepdims=True) + w_inter * q_n
    h = num / jnp.maximum(jnp.abs(den), jnp.exp(-m_new))

    m_last = [m_new[(g + 1) * ls - 1:(g + 1) * ls, :] for g in range(G)]
    b_last = [b_c[(g + 1) * ls - 1:(g + 1) * ls, :] for g in range(G)]
    wl_c = jnp.exp(per_seq(b_last) - b_c + ig_c - per_seq(m_last))
    wk = k.astype(F32) * wl_c
    new_states = []
    for g, (C, n, m) in enumerate(states):
        wk_g = wk if G == 1 else jnp.where(seg_c == g, wk, 0.0)
        wl_inter = jnp.exp(b_last[g] + m - m_last[g])
        kv = lax.dot_general(wk_g.astype(BF16), v, (((0,), (0,)), ((), ())),
                             preferred_element_type=F32)
        new_states.append((wl_inter * C + kv,
                           wl_inter * n + jnp.sum(wk_g, axis=0, keepdims=True),
                           m_last[g]))
    return h, new_states


def _head_out(h, gh, o):
    return _rmsnorm(h, gh) * jax.nn.sigmoid(o.astype(F32))


def _gate_cols(g_c, g_r, bc_ref, br_ref):
    pre_c = g_c + br_ref[...]
    pre_r = g_r + bc_ref[...]
    return pre_c, _log_sigmoid(pre_c), pre_r, _log_sigmoid(pre_r)


def _mlstm_prompt_kernel(q_ref, k_ref, v_ref, o_ref, gc_ref, gr_ref, bc_ref, br_ref, gh_ref,
                         hb_ref, C_out, n_out, m_out, C_s, n_s, m_s, *, nh, d):
    c = pl.program_id(1)

    @pl.when(c == 0)
    def _():
        C_s[...] = jnp.zeros(C_s.shape, F32)
        n_s[...] = jnp.zeros(n_s.shape, F32)
        m_s[...] = jnp.zeros(m_s.shape, F32)

    L = q_ref.shape[0]
    pre_c, lf_call, pre_r, lf_rall = _gate_cols(gc_ref[...], gr_ref[...], bc_ref, br_ref)
    for hd in range(nh):
        cols = slice(hd * d, (hd + 1) * d)
        st = (C_s[hd], n_s[hd], m_s[hd])
        h, (new,) = _mlstm_cell(
            q_ref[:, cols], k_ref[:, cols], v_ref[:, cols],
            pre_c[:, hd:hd + 1], pre_r[hd:hd + 1, :],
            lf_call[:, nh + hd:nh + hd + 1], lf_rall[nh + hd:nh + hd + 1, :],
            [st], L)
        C_s[hd], n_s[hd], m_s[hd] = new
        hb_ref[:, cols] = _head_out(h, gh_ref[:, cols], o_ref[:, cols]).astype(hb_ref.dtype)

    @pl.when(c == pl.num_programs(1) - 1)
    def _():
        C_out[0] = C_s[...]
        n_out[0] = n_s[...]
        m_out[0] = m_s[...]


def _mlstm_prompt(qkvo, gates_c, gates_r, b_if, g_head, B, S, nh, chunk=256):
    W = qkvo.shape[1] // 4
    d = W // nh
    L = _tile(S, chunk)
    nc = S // L
    tok = lambda j: pl.BlockSpec((L, W), lambda b, c, j=j: (b * nc + c, j))
    const = lambda shape: pl.BlockSpec(shape, lambda b, c: (0,) * len(shape))
    return pl.pallas_call(
        functools.partial(_mlstm_prompt_kernel, nh=nh, d=d),
        grid=(B, nc),
        in_specs=[tok(0), tok(1), tok(2), tok(3),
                  pl.BlockSpec((L, 2 * nh), lambda b, c: (b * nc + c, 0)),
                  pl.BlockSpec((2 * nh, L), lambda b, c: (0, b * nc + c)),
                  const((2 * nh, 1)), const((1, 2 * nh)), const((1, W))],
        out_specs=[pl.BlockSpec((L, W), lambda b, c: (b * nc + c, 0)),
                   pl.BlockSpec((1, nh, d, d), lambda b, c: (b, 0, 0, 0)),
                   pl.BlockSpec((1, nh, 1, d), lambda b, c: (b, 0, 0, 0)),
                   pl.BlockSpec((1, nh, 1, 1), lambda b, c: (b, 0, 0, 0))],
        out_shape=[jax.ShapeDtypeStruct((B * S, W), BF16),
                   jax.ShapeDtypeStruct((B, nh, d, d), F32),
                   jax.ShapeDtypeStruct((B, nh, 1, d), F32),
                   jax.ShapeDtypeStruct((B, nh, 1, 1), F32)],
        scratch_shapes=[pltpu.VMEM((nh, d, d), F32), pltpu.VMEM((nh, 1, d), F32),
                        pltpu.VMEM((nh, 1, 1), F32)],
        compiler_params=_cparams(2),
        name="mlstm_prompt",
    )(qkvo, qkvo, qkvo, qkvo, gates_c, gates_r, b_if.reshape(2 * nh, 1), b_if.reshape(1, 2 * nh),
      g_head.reshape(1, W))


def _mlstm_sample_kernel(q_ref, k_ref, v_ref, o_ref, gc_ref, gr_ref, bc_ref, br_ref, gh_ref,
                         C_in, n_in, m_in, hb_ref, C_out, n_out, m_out, *, nh, d, bb, sq):
    pre_c, lf_call, pre_r, lf_rall = _gate_cols(gc_ref[...], gr_ref[0], bc_ref, br_ref)
    for hd in range(nh):
        cols = slice(hd * d, (hd + 1) * d)
        states = [(C_in[j, hd], n_in[j, hd], m_in[j, hd]) for j in range(bb)]
        h, new = _mlstm_cell(
            q_ref[:, cols], k_ref[:, cols], v_ref[:, cols],
            pre_c[:, hd:hd + 1], pre_r[hd:hd + 1, :],
            lf_call[:, nh + hd:nh + hd + 1], lf_rall[nh + hd:nh + hd + 1, :],
            states, sq)
        for j in range(bb):
            C_out[j, hd], n_out[j, hd], m_out[j, hd] = new[j]
        hb_ref[:, cols] = _head_out(h, gh_ref[:, cols], o_ref[:, cols]).astype(hb_ref.dtype)


def _mlstm_sample(qkvo, gates_c, gates_r, b_if, g_head, C0, n0, m0, row0, Bs, sq, nh, bb=2):
    W = qkvo.shape[1] // 4
    d = W // nh
    L = bb * sq
    blk0 = row0 // L
    tok = lambda j: pl.BlockSpec((L, W), lambda i, j=j: (blk0 + i, j))
    const = lambda shape: pl.BlockSpec(shape, lambda i: (0,) * len(shape))
    st = lambda a, b: pl.BlockSpec((bb, nh, a, b), lambda i: (i, 0, 0, 0))
    return pl.pallas_call(
        functools.partial(_mlstm_sample_kernel, nh=nh, d=d, bb=bb, sq=sq),
        grid=(Bs // bb,),
        in_specs=[tok(0), tok(1), tok(2), tok(3),
                  pl.BlockSpec((L, 2 * nh), lambda i: (blk0 + i, 0)),
                  pl.BlockSpec((1, 2 * nh, L), lambda i: (i, 0, 0)),
                  const((2 * nh, 1)), const((1, 2 * nh)), const((1, W)),
                  st(d, d), st(1, d), st(1, 1)],
        out_specs=[pl.BlockSpec((L, W), lambda i: (i, 0)), st(d, d), st(1, d), st(1, 1)],
        out_shape=[jax.ShapeDtypeStruct((Bs * sq, W), BF16),
                   jax.ShapeDtypeStruct((Bs, nh, d, d), F32),
                   jax.ShapeDtypeStruct((Bs, nh, 1, d), F32),
                   jax.ShapeDtypeStruct((Bs, nh, 1, 1), F32)],
        compiler_params=_cparams(1),
        name="mlstm_sample",
    )(qkvo, qkvo, qkvo, qkvo, gates_c, gates_r, b_if.reshape(2 * nh, 1), b_if.reshape(1, 2 * nh),
      g_head.reshape(1, W), C0, n0.reshape(Bs, nh, 1, d), m0.reshape(Bs, nh, 1, 1))


def _merge_kernel(ap_ref, as_ref, hp_ref, hs_ref, ga_ref, gb_ref, wa_ref, wb_ref, o_ref,
                  wa_bf, wb_bf, *, n_first):
    m = pl.program_id(1)

    @pl.when(m == 0)
    def _():
        wa_bf[...] = wa_ref[...].astype(BF16)
        wb_bf[...] = wb_ref[...].astype(BF16)

    def body(a, hb):
        pa = jnp.dot(a, wa_bf[...], preferred_element_type=F32)
        pb = jnp.dot(hb, wb_bf[...], preferred_element_type=F32)
        mix = (jax.nn.sigmoid(ga_ref[...].astype(F32)) * pa
               + jax.nn.sigmoid(gb_ref[...].astype(F32)) * pb)
        o_ref[...] = mix.astype(o_ref.dtype)

    @pl.when(m < n_first)
    def _():
        body(ap_ref[...], hp_ref[...])

    @pl.when(m >= n_first)
    def _():
        body(as_ref[...], hs_ref[...])


def _merge(a_p, a_s, hb_p, hb_s, gab, wa, wb, tm=512, tn=1024):
    Tp, P = a_p.shape
    T = Tp + a_s.shape[0]
    W = hb_p.shape[1]
    D = wa.shape[1]
    tm = _tile(math.gcd(Tp, T - Tp), tm)
    tn = _tile(D, tn)
    nn = D // tn
    n_first = Tp // tm
    a_specs = _split_specs((tm, P), n_first, lambda n, m: m)
    h_specs = _split_specs((tm, W), n_first, lambda n, m: m)
    return pl.pallas_call(
        functools.partial(_merge_kernel, n_first=n_first),
        grid=(nn, T // tm),
        in_specs=[*a_specs, *h_specs,
                  pl.BlockSpec((tm, tn), lambda n, m: (m, n)),
                  pl.BlockSpec((tm, tn), lambda n, m: (m, nn + n)),
                  pl.BlockSpec((P, tn), lambda n, m: (0, n)),
                  pl.BlockSpec((W, tn), lambda n, m: (0, n))],
        out_specs=pl.BlockSpec((tm, tn), lambda n, m: (m, n)),
        out_shape=jax.ShapeDtypeStruct((T, D), BF16),
        scratch_shapes=[pltpu.VMEM((P, tn), BF16), pltpu.VMEM((W, tn), BF16)],
        compiler_params=_cparams(2),
        name="merge",
    )(a_p, a_s, hb_p, hb_s, gab, gab, wa, wb)


def _moe_local_rows(tt):
    return -(-(2 * tt + N_EXPERTS * (SUBLANES - 1)) // LANES) * LANES


def _router_kernel(x_ref, g_ref, w_ref, b_ref, xl_ref, meta_ref, cnt_ref, *, lr):
    tt = x_ref.shape[0]
    xn = _rmsnorm(x_ref[...], g_ref[...])
    logits = _dot_split(xn, w_ref[...]) + b_ref[...]
    lane = lax.broadcasted_iota(I32, logits.shape, 1).astype(F32)
    neg = -jnp.inf
    gl = jnp.where(lane < N_EXPERT_GROUPS, logits, neg)
    gmax = jnp.max(gl, axis=1, keepdims=True)
    g_idx = jnp.min(jnp.where(gl == gmax, lane, float(LANES)), axis=1, keepdims=True)
    g_val = 1.0 / jnp.sum(jnp.exp(gl - gmax), axis=1, keepdims=True)
    lo = N_EXPERT_GROUPS + g_idx * EXPERTS_PER_GROUP
    el = jnp.where((lane >= lo) & (lane < lo + EXPERTS_PER_GROUP), logits, neg)
    t1 = jnp.max(el, axis=1, keepdims=True)
    i1 = jnp.min(jnp.where(el == t1, lane, float(LANES)), axis=1, keepdims=True)
    el2 = jnp.where(lane == i1, neg, el)
    t2 = jnp.max(el2, axis=1, keepdims=True)
    i2 = jnp.min(jnp.where(el2 == t2, lane, float(LANES)), axis=1, keepdims=True)
    e2 = jnp.exp(t2 - t1)
    w1 = g_val / (1.0 + e2)
    w2 = g_val * e2 / (1.0 + e2)

    oh1 = (lane == i1 - N_EXPERT_GROUPS).astype(F32)
    oh2 = (lane == i2 - N_EXPERT_GROUPS).astype(F32)
    r_i = lax.broadcasted_iota(I32, (tt, tt), 0)
    c_i = lax.broadcasted_iota(I32, (tt, tt), 1)
    before = (c_i < r_i).astype(BF16)
    rank1 = jnp.dot(before, oh1.astype(BF16), preferred_element_type=F32)
    rank2 = jnp.dot(before, oh2.astype(BF16), preferred_element_type=F32)
    cnt1 = jnp.sum(oh1, axis=0, keepdims=True)
    cnt = cnt1 + jnp.sum(oh2, axis=0, keepdims=True)
    cnt8 = jnp.floor((cnt + (SUBLANES - 1)) * (1.0 / SUBLANES)) * SUBLANES
    e_r = lax.broadcasted_iota(I32, (LANES, LANES), 0)
    e_c = lax.broadcasted_iota(I32, (LANES, LANES), 1)
    start = jnp.dot(jnp.broadcast_to(cnt8, (2 * SUBLANES, LANES)).astype(BF16),
                    (e_r < e_c).astype(BF16), preferred_element_type=F32)[0:1]
    row1 = jnp.sum(oh1 * (start + rank1), axis=1, keepdims=True)
    row2 = jnp.sum(oh2 * (start + cnt1 + rank2), axis=1, keepdims=True)
    meta = jnp.where(lane == 0.0, row1, jnp.where(lane == 1.0, row2,
                     jnp.where(lane == 2.0, w1, jnp.where(lane == 3.0, w2, 0.0))))
    meta_ref[...] = meta
    cnt_ref[0] = cnt8.astype(I32)

    meta_t = meta.T
    dst = lax.broadcasted_iota(I32, (lr, tt), 0).astype(F32)
    place = ((dst == meta_t[0:1, :]) | (dst == meta_t[1:2, :])).astype(BF16)
    xl_ref[...] = jnp.dot(place, xn.astype(BF16), preferred_element_type=F32)


def _router(x1, g_ffn, w_r, b_r, tt):
    T, D = x1.shape
    lr = _moe_local_rows(tt)
    nj = T // tt
    return pl.pallas_call(
        functools.partial(_router_kernel, lr=lr),
        grid=(nj,),
        in_specs=[pl.BlockSpec((tt, D), lambda i: (i, 0)),
                  pl.BlockSpec((1, D), lambda i: (0, 0)),
                  pl.BlockSpec((D, 2 * LANES), lambda i: (0, 0)),
                  pl.BlockSpec((1, LANES), lambda i: (0, 0))],
        out_specs=[pl.BlockSpec((lr, D), lambda i: (i, 0)),
                   pl.BlockSpec((tt, LANES), lambda i: (i, 0)),
                   pl.BlockSpec((1, 1, LANES), lambda i: (i, 0, 0))],
        out_shape=[jax.ShapeDtypeStruct((nj * lr, D), F32),
                   jax.ShapeDtypeStruct((T, LANES), F32),
                   jax.ShapeDtypeStruct((nj, 1, LANES), I32)],
        compiler_params=_cparams(1),
        name="moe_route",
    )(x1, g_ffn.reshape(1, D), w_r, b_r)


def _moe_plan(cnt, tt, lr, tr):
    nj, E = cnt.shape
    n_tiles = nj * (2 * tt + E * (SUBLANES - 1)) // tr + E
    pre = jnp.cumsum(cnt, axis=0) - cnt
    tot = jnp.sum(cnt, axis=0)
    lstart = jnp.cumsum(cnt, axis=1) - cnt
    nt_e = (tot + tr - 1) // tr
    t_end = jnp.cumsum(nt_e)
    t_start = t_end - nt_e
    tile = jnp.arange(n_tiles, dtype=I32)
    tile_e = jnp.minimum(jnp.sum((t_end[None, :] <= tile[:, None]).astype(I32), axis=1), E - 1)
    tile_a = (tile - t_start[tile_e]) * tr
    tile_next = t_end[tile_e]
    n_used = t_end[-1].reshape(1)
    src_local = jnp.arange(nj, dtype=I32)[:, None] * lr + lstart
    src_global = t_start[None, :] * tr + pre
    as_i32 = lambda a: a.astype(I32)
    by_expert = lambda a: as_i32(a.T.reshape(-1))
    by_tile = lambda a: as_i32(a.reshape(-1))
    return dict(tile_e=as_i32(tile_e), tile_a=as_i32(tile_a), tile_next=as_i32(tile_next),
                n_used=as_i32(n_used),
                pre_e=by_expert(pre), cnt_e=by_expert(cnt), src_e=by_expert(src_local),
                cnt_j=by_tile(cnt), dst_j=by_tile(lstart), src_j=by_tile(src_global),
                n_tiles=n_tiles)


def _experts_kernel(te_ref, ta_ref, tn_ref, nu_ref, pre_ref, cnt_ref, src_ref,
                    xl_hbm, wg_hbm, wu_hbm, wd_hbm, o_ref,
                    xbuf, wg_st, wu_st, wd_st, wg_bf, wu_bf, wd_bf, sem, wsem, *, nj, tr):
    i = pl.program_id(0)
    n_used = nu_ref[0]

    def weight_copies(e):
        return (pltpu.make_async_copy(wg_hbm.at[e], wg_st, wsem.at[0]),
                pltpu.make_async_copy(wu_hbm.at[e], wu_st, wsem.at[1]),
                pltpu.make_async_copy(wd_hbm.at[e], wd_st, wsem.at[2]))

    def for_each_run(tile, slot, fn):
        e = te_ref[tile]
        a = ta_ref[tile]

        def body(j, c):
            lo = pre_ref[e * nj + j]
            first = jnp.maximum(lo, a)
            n = jnp.minimum(lo + cnt_ref[e * nj + j], a + tr) - first

            @pl.when(n > 0)
            def _():
                src = pl.multiple_of(src_ref[e * nj + j] + (first - lo), SUBLANES)
                dst = pl.multiple_of(first - a, SUBLANES)
                rows = pl.multiple_of(n, SUBLANES)
                fn(pltpu.make_async_copy(xl_hbm.at[pl.ds(src, rows)],
                                         xbuf.at[slot, pl.ds(dst, rows)], sem.at[slot]))
            return c

        lax.fori_loop(0, nj, body, 0)

    @pl.when((i == 0) & (n_used > 0))
    def _():
        xbuf[...] = jnp.zeros(xbuf.shape, xbuf.dtype)
        for_each_run(0, 0, lambda cp: cp.start())
        for cp in weight_copies(te_ref[0]):
            cp.start()

    @pl.when(i + 1 < n_used)
    def _():
        for_each_run(i + 1, (i + 1) % 2, lambda cp: cp.start())

    @pl.when(i < n_used)
    def _():
        @pl.when((i == 0) | (te_ref[i] != te_ref[jnp.maximum(i - 1, 0)]))
        def _():
            for cp in weight_copies(te_ref[i]):
                cp.wait()
            wg_bf[...] = wg_st[...].astype(BF16)
            wu_bf[...] = wu_st[...].astype(BF16)
            wd_bf[...] = wd_st[...].astype(BF16)
            nxt = tn_ref[i]

            @pl.when(nxt < n_used)
            def _():
                for cp in weight_copies(te_ref[nxt]):
                    cp.start()

        for_each_run(i, i % 2, lambda cp: cp.wait())
        x = xbuf[i % 2].astype(BF16)
        hg = jnp.dot(x, wg_bf[...], preferred_element_type=F32)
        hu = jnp.dot(x, wu_bf[...], preferred_element_type=F32)
        h = hg * jax.nn.sigmoid(hg) * hu
        o_ref[...] = jnp.dot(h.astype(BF16), wd_bf[...], preferred_element_type=F32)

    @pl.when(i >= n_used)
    def _():
        o_ref[...] = jnp.zeros(o_ref.shape, o_ref.dtype)


def _experts(xl, plan, w_eg, w_eu, w_ed, nj, tr):
    D = xl.shape[1]
    E, _, FF = w_eg.shape
    n_tiles = plan["n_tiles"]
    any_space = pl.BlockSpec(memory_space=pl.ANY)
    return pl.pallas_call(
        functools.partial(_experts_kernel, nj=nj, tr=tr),
        grid_spec=pltpu.PrefetchScalarGridSpec(
            num_scalar_prefetch=7,
            grid=(n_tiles,),
            in_specs=[any_space, any_space, any_space, any_space],
            out_specs=pl.BlockSpec((tr, D), lambda i, *_: (i, 0)),
            scratch_shapes=[pltpu.VMEM((2, tr, D), F32),
                            pltpu.VMEM((D, FF), F32), pltpu.VMEM((D, FF), F32),
                            pltpu.VMEM((FF, D), F32),
                            pltpu.VMEM((D, FF), BF16), pltpu.VMEM((D, FF), BF16),
                            pltpu.VMEM((FF, D), BF16),
                            pltpu.SemaphoreType.DMA((2,)), pltpu.SemaphoreType.DMA((3,))]),
        out_shape=jax.ShapeDtypeStruct((n_tiles * tr, D), F32),
        compiler_params=_cparams(1),
        name="moe_experts",
    )(plan["tile_e"], plan["tile_a"], plan["tile_next"], plan["n_used"], plan["pre_e"],
      plan["cnt_e"], plan["src_e"], xl, w_eg, w_eu, w_ed)


def _combine_kernel(cnt_ref, dst_ref, src_ref, x1_ref, meta_ref, g_ref, ys_hbm, yp_ref, ys_ref,
                    ybuf, sem, *, n_first):
    j = pl.program_id(0)
    nj = pl.num_programs(0)
    lr = ybuf.shape[1]

    def for_each_run(tile, slot, fn):
        def body(e, c):
            n = cnt_ref[tile * N_EXPERTS + e]

            @pl.when(n > 0)
            def _():
                src = pl.multiple_of(src_ref[tile * N_EXPERTS + e], SUBLANES)
                dst = pl.multiple_of(dst_ref[tile * N_EXPERTS + e], SUBLANES)
                rows = pl.multiple_of(n, SUBLANES)
                fn(pltpu.make_async_copy(ys_hbm.at[pl.ds(src, rows)],
                                         ybuf.at[slot, pl.ds(dst, rows)], sem.at[slot]))
            return c

        lax.fori_loop(0, N_EXPERTS, body, 0)

    @pl.when(j == 0)
    def _():
        ybuf[...] = jnp.zeros(ybuf.shape, ybuf.dtype)
        for_each_run(0, 0, lambda cp: cp.start())

    @pl.when(j + 1 < nj)
    def _():
        for_each_run(j + 1, (j + 1) % 2, lambda cp: cp.start())

    for_each_run(j, j % 2, lambda cp: cp.wait())
    meta = meta_ref[...]
    col = lax.broadcasted_iota(I32, (meta.shape[0], lr), 1).astype(F32)
    pick = jnp.where(col == meta[:, 0:1], meta[:, 2:3],
                     jnp.where(col == meta[:, 1:2], meta[:, 3:4], 0.0))
    moe = jnp.dot(pick.astype(BF16), ybuf[j % 2].astype(BF16), preferred_element_type=F32)
    y = _rmsnorm(x1_ref[...] + moe, g_ref[...])

    @pl.when(j < n_first)
    def _():
        yp_ref[...] = y

    @pl.when(j >= n_first)
    def _():
        ys_ref[...] = y


def _combine(x1, meta, g_final, ys, plan, Tp, tt, lr):
    T, D = x1.shape
    nj = T // tt
    n_first = Tp // tt
    out_p, out_s = _split_specs((tt, D), n_first, lambda j, *_: j)
    return pl.pallas_call(
        functools.partial(_combine_kernel, n_first=n_first),
        grid_spec=pltpu.PrefetchScalarGridSpec(
            num_scalar_prefetch=3,
            grid=(nj,),
            in_specs=[pl.BlockSpec((tt, D), lambda j, *_: (j, 0)),
                      pl.BlockSpec((tt, LANES), lambda j, *_: (j, 0)),
                      pl.BlockSpec((1, D), lambda j, *_: (0, 0)),
                      pl.BlockSpec(memory_space=pl.ANY)],
            out_specs=[out_p, out_s],
            scratch_shapes=[pltpu.VMEM((2, lr, D), F32), pltpu.SemaphoreType.DMA((2,))]),
        out_shape=[jax.ShapeDtypeStruct((Tp, D), F32), jax.ShapeDtypeStruct((T - Tp, D), F32)],
        compiler_params=_cparams(1),
        name="moe_combine",
    )(plan["cnt_j"], plan["dst_j"], plan["src_j"], x1, meta, g_final.reshape(1, D), ys)


def _pad_cols(w, n):
    return jnp.pad(w, ((0, 0), (0, n - w.shape[1])))


def _layer(xp, xs, Bp, Sp, Bs, Ss, state_pool, state_C, state_n, state_m, g_mix, w_in, b_if,
           w_pool, pool_scale, w_proj_a, w_proj_b, g_head, w_out, g_ffn, w_rg, b_rg, w_re, b_re,
           w_eg, w_eu, w_ed, g_out):
    D = xp.shape[1]
    Tp = Bp * Sp
    T = Tp + Bs * Ss
    P = w_pool.shape[0] * w_pool.shape[1]
    nh = b_if.shape[0] // 2
    W = w_proj_b.shape[0]
    d = W // nh
    gate0 = P + 4 * W

    w_in_t = w_in.T
    xn, gates = _rms_proj(xp, xs, g_mix, w_in_t[gate0:gate0 + 2 * nh].T)
    gates = gates[:, :2 * nh]
    u = _matmul_wt(xn, w_in_t, 0, P, F32)
    k_scale = jnp.concatenate([jnp.ones((W,), F32), jnp.full((W,), d ** -0.5, F32),
                               jnp.ones((2 * W,), F32)])
    qkvo = _matmul_wt(xn, w_in_t, P, 4 * W, BF16, scale=k_scale)
    gab = _matmul_wt(xn, w_in_t, gate0 + 2 * nh, 2 * D, BF16)

    wp_bf = w_pool.astype(BF16)
    nbuf = state_pool.shape[1]
    hist = jnp.pad(state_pool, ((0, 0), (POOL_HIST - nbuf, 0), (0, 0)))
    a_p = _pool_prompt(u, wp_bf, pool_scale, Bp, Sp)
    a_s = _pool_sample(u, hist, wp_bf, pool_scale, Tp, Bs, Ss)
    pool_p = jnp.stack([u[(b + 1) * Sp - nbuf:(b + 1) * Sp] for b in range(Bp)])
    pool_s = jnp.concatenate([state_pool, u[Tp:].reshape(Bs, Ss, P)], axis=1)[:, -nbuf:]

    gates_r = gates.T
    hb_p, C_p, n_p, m_p = _mlstm_prompt(qkvo, gates, gates_r, b_if, g_head, Bp, Sp, nh)
    bb = 2
    gr_s = gates_r[:, Tp:].reshape(2 * nh, Bs // bb, bb * Ss).transpose(1, 0, 2)
    hb_s, C_s, n_s, m_s = _mlstm_sample(qkvo, gates, gr_s, b_if, g_head, state_C, state_n,
                                        state_m, Tp, Bs, Ss, nh, bb=bb)

    mix = _merge(a_p, a_s, hb_p, hb_s, gab, w_proj_a, w_proj_b)
    x1 = _matmul(mix, w_out, F32, res=(xp, xs))

    tt = _tile(math.gcd(Tp, T - Tp), MOE_TT)
    lr = _moe_local_rows(tt)
    w_r = _split_weight(jnp.concatenate([w_rg, w_re], axis=1))
    b_r = _pad_cols(jnp.concatenate([b_rg, b_re]).reshape(1, -1), LANES)
    xl, meta, cnt = _router(x1, g_ffn, w_r, b_r, tt)
    plan = _moe_plan(cnt[:, 0, :N_EXPERTS], tt, lr, MOE_TR)
    ys = _experts(xl, plan, w_eg, w_eu, w_ed, T // tt, MOE_TR)
    y_p, y_s = _combine(x1, meta, g_out, ys, plan, Tp, tt, lr)
    states_p = (pool_p, C_p, n_p.reshape(Bp, nh, d), m_p.reshape(Bp, nh))
    states_s = (pool_s, C_s, n_s.reshape(Bs, nh, d), m_s.reshape(Bs, nh))
    return y_p, y_s, states_p, states_s


def kernel(x_prompt, x_sample, state_pool, state_C, state_n, state_m, g_mix, w_in, b_if, w_pool, pool_scale, w_proj_a, w_proj_b, g_head, w_out, g_ffn, w_router_group, b_router_group, w_router_expert, b_router_expert, w_exp_gate, w_exp_up, w_exp_down, g_final):
    Bp, Sp, D = x_prompt.shape
    Bs, Ss, _ = x_sample.shape
    assert g_mix.shape[0] == 1, "the closing norm is fused into the layer's last kernel"
    l = 0
    y_p, y_s, sp, ss = _layer(
        x_prompt.reshape(Bp * Sp, D), x_sample.reshape(Bs * Ss, D), Bp, Sp, Bs, Ss,
        state_pool[l], state_C[l], state_n[l], state_m[l], g_mix[l], w_in[l], b_if[l], w_pool[l],
        pool_scale[l], w_proj_a[l], w_proj_b[l], g_head[l], w_out[l], g_ffn[l],
        w_router_group[l], b_router_group[l], w_router_expert[l], b_router_expert[l],
        w_exp_gate[l], w_exp_up[l], w_exp_down[l], g_final)
    return (y_p.reshape(Bp, Sp, D), y_s.reshape(Bs, Ss, D),
            sp[0][None], sp[1][None], sp[2][None], sp[3][None],
            ss[0][None], ss[1][None], ss[2][None], ss[3][None])
```

```python
import functools
import math

import jax
import jax.numpy as jnp
from jax import lax
from jax.experimental import pallas as pl
from jax.experimental.pallas import tpu as pltpu

F32 = jnp.float32
BF16 = jnp.bfloat16
I32 = jnp.int32

RMS_EPS = 1e-6
POOL_WINDOWS = (2, 4, 8, 16)
POOL_HIST = 16
SAMPLE_POS0 = 16384
N_EXPERT_GROUPS = 4
EXPERTS_PER_GROUP = 8
N_EXPERTS = N_EXPERT_GROUPS * EXPERTS_PER_GROUP
LANES = 128
SUBLANES = 8
VMEM_LIMIT = 56 * 1024 * 1024
MOE_TT = 256
MOE_TR = 256


def _cparams(n_axes):
    return pltpu.CompilerParams(dimension_semantics=("arbitrary",) * n_axes,
                                vmem_limit_bytes=VMEM_LIMIT)


def _tile(n, pref):
    t = min(n, pref)
    while n % t:
        t -= 1
    return t


def _split_specs(shape, n_first, axis_fn):
    first = pl.BlockSpec(shape, lambda *g: (jnp.minimum(axis_fn(*g), n_first - 1), 0))
    second = pl.BlockSpec(shape, lambda *g: (jnp.maximum(axis_fn(*g) - n_first, 0), 0))
    return first, second


def _rmsnorm(x, g):
    return x * lax.rsqrt(jnp.mean(x * x, axis=-1, keepdims=True) + RMS_EPS) * g


def _split_weight(w):
    w = _pad_cols(w, LANES)
    hi = w.astype(BF16)
    lo = (w - hi.astype(F32)).astype(BF16)
    return jnp.concatenate([hi, lo], axis=1)


def _dot_split(x, w_hl):
    xh = x.astype(BF16)
    xl = (x - xh.astype(F32)).astype(BF16)
    p = jnp.dot(xh, w_hl, preferred_element_type=F32)
    return (p[:, :LANES] + p[:, LANES:]
            + jnp.dot(xl, w_hl[:, :LANES], preferred_element_type=F32))


def _rms_proj_kernel(xp_ref, xs_ref, g_ref, w_ref, xn_ref, p_ref, *, n_first):
    def body(x):
        xn = _rmsnorm(x, g_ref[...])
        xn_ref[...] = xn.astype(xn_ref.dtype)
        p_ref[...] = _dot_split(xn, w_ref[...])

    i = pl.program_id(0)

    @pl.when(i < n_first)
    def _():
        body(xp_ref[...])

    @pl.when(i >= n_first)
    def _():
        body(xs_ref[...])


def _rms_proj(xp, xs, g, w_small, tm=512):
    Tp, D = xp.shape
    Ts = xs.shape[0]
    tm = _tile(math.gcd(Tp, Ts), tm)
    T = Tp + Ts
    sp, ss = _split_specs((tm, D), Tp // tm, lambda i: i)
    return pl.pallas_call(
        functools.partial(_rms_proj_kernel, n_first=Tp // tm),
        grid=(T // tm,),
        in_specs=[sp, ss,
                  pl.BlockSpec((1, D), lambda i: (0, 0)),
                  pl.BlockSpec((D, 2 * LANES), lambda i: (0, 0))],
        out_specs=[pl.BlockSpec((tm, D), lambda i: (i, 0)),
                   pl.BlockSpec((tm, LANES), lambda i: (i, 0))],
        out_shape=[jax.ShapeDtypeStruct((T, D), BF16),
                   jax.ShapeDtypeStruct((T, LANES), F32)],
        compiler_params=_cparams(1),
        name="rms_proj",
    )(xp, xs, g.reshape(1, D), _split_weight(w_small))


def _mm_kernel(*refs, n_res_first):
    x_ref, w_ref = refs[0], refs[1]
    res_refs = refs[2:4] if n_res_first is not None else ()
    o_ref, wbf_ref = refs[-2:]

    @pl.when(pl.program_id(1) == 0)
    def _():
        wbf_ref[...] = w_ref[...].astype(BF16)

    acc = jnp.dot(x_ref[...], wbf_ref[...], preferred_element_type=F32)
    if n_res_first is None:
        o_ref[...] = acc.astype(o_ref.dtype)
    else:
        m = pl.program_id(1)

        @pl.when(m < n_res_first)
        def _():
            o_ref[...] = (acc + res_refs[0][...]).astype(o_ref.dtype)

        @pl.when(m >= n_res_first)
        def _():
            o_ref[...] = (acc + res_refs[1][...]).astype(o_ref.dtype)


def _matmul(x, w, out_dtype, res=None, tm=512, tn=1024):
    T, K = x.shape
    ncols = w.shape[1]
    tn = _tile(ncols, tn)
    if res is not None:
        tm = _tile(math.gcd(res[0].shape[0], res[1].shape[0]), tm)
    tm = _tile(T, tm)
    in_specs = [pl.BlockSpec((tm, K), lambda n, m: (m, 0)),
                pl.BlockSpec((K, tn), lambda n, m: (0, n))]
    args = [x, w]
    n_res_first = None
    if res is not None:
        n_res_first = res[0].shape[0] // tm
        first = pl.BlockSpec((tm, tn), lambda n, m: (jnp.minimum(m, n_res_first - 1), n))
        second = pl.BlockSpec((tm, tn), lambda n, m: (jnp.maximum(m - n_res_first, 0), n))
        in_specs += [first, second]
        args += list(res)
    return pl.pallas_call(
        functools.partial(_mm_kernel, n_res_first=n_res_first),
        grid=(ncols // tn, T // tm),
        in_specs=in_specs,
        out_specs=pl.BlockSpec((tm, tn), lambda n, m: (m, n)),
        out_shape=jax.ShapeDtypeStruct((T, ncols), out_dtype),
        scratch_shapes=[pltpu.VMEM((K, tn), BF16)],
        compiler_params=_cparams(2),
        name="mm_wstat",
    )(*args)


def _mm_wt_kernel(*refs, has_scale, row0, tn):
    x_ref, wt_hbm = refs[0], refs[1]
    scale_ref = refs[2] if has_scale else None
    o_ref, stage, wbf_ref, sem = refs[-4:]
    n = pl.program_id(0)
    nn = pl.num_programs(0)

    def w_copy(tile, slot):
        start = pl.multiple_of(row0 + tile * tn, SUBLANES)
        return pltpu.make_async_copy(wt_hbm.at[pl.ds(start, tn)], stage.at[slot], sem.at[slot])

    @pl.when(pl.program_id(1) == 0)
    def _():
        @pl.when(n == 0)
        def _():
            w_copy(0, 0).start()

        w_copy(n, n % 2).wait()

        @pl.when(n + 1 < nn)
        def _():
            w_copy(n + 1, (n + 1) % 2).start()

        wbf_ref[...] = stage[n % 2].T.astype(BF16)

    acc = jnp.dot(x_ref[...], wbf_ref[...], preferred_element_type=F32)
    if has_scale:
        acc = acc * scale_ref[...]
    o_ref[...] = acc.astype(o_ref.dtype)


def _matmul_wt(x, w_t, row0, ncols, out_dtype, scale=None, tm=1024, tn=1024):
    T, K = x.shape
    assert row0 % SUBLANES == 0
    tm = _tile(T, tm)
    tn = _tile(ncols, tn)
    in_specs = [pl.BlockSpec((tm, K), lambda n, m: (m, 0)),
                pl.BlockSpec(memory_space=pl.ANY)]
    args = [x, w_t]
    if scale is not None:
        in_specs.append(pl.BlockSpec((1, tn), lambda n, m: (0, n)))
        args.append(scale.reshape(1, ncols))
    return pl.pallas_call(
        functools.partial(_mm_wt_kernel, has_scale=scale is not None, row0=row0, tn=tn),
        grid=(ncols // tn, T // tm),
        in_specs=in_specs,
        out_specs=pl.BlockSpec((tm, tn), lambda n, m: (m, n)),
        out_shape=jax.ShapeDtypeStruct((T, ncols), out_dtype),
        scratch_shapes=[pltpu.VMEM((2, tn, K), F32), pltpu.VMEM((K, tn), BF16),
                        pltpu.SemaphoreType.DMA((2,))],
        compiler_params=_cparams(2),
        name="mm_wt",
    )(*args)


def _pool_prompt_kernel(u_ref, wp_ref, ps_ref, a_ref, ext_ref, *, ts, gc, pos0):
    s = pl.program_id(1)

    @pl.when(s == 0)
    def _():
        ext_ref[0:POOL_HIST, :] = jnp.zeros((POOL_HIST, ext_ref.shape[1]), F32)

    ext_ref[POOL_HIST:, :] = u_ref[...]
    t_abs = pos0 + s * ts + lax.broadcasted_iota(I32, (ts, 1), 0)
    for g, w in enumerate(POOL_WINDOWS):
        cols = slice(g * gc, (g + 1) * gc)
        e = ext_ref[:, cols]
        acc = e
        span = 1
        while span < w:
            acc = acc + pltpu.roll(acc, span, 0)
            span *= 2
        inv = 1.0 / jnp.minimum(t_abs + 1, w).astype(F32)
        pooled = acc[POOL_HIST:, :] * inv - e[POOL_HIST:, :]
        mixed = jnp.dot(pooled.astype(BF16), wp_ref[g], preferred_element_type=F32)
        a_ref[:, cols] = (mixed * ps_ref[:, cols]).astype(a_ref.dtype)
    ext_ref[0:POOL_HIST, :] = ext_ref[ts:ts + POOL_HIST, :]


def _pool_prompt(u_all, wp_bf, pool_scale, B, S, ts=512):
    P = u_all.shape[1]
    ts = _tile(S, ts)
    nst = S // ts
    G = len(POOL_WINDOWS)
    gc = P // G
    return pl.pallas_call(
        functools.partial(_pool_prompt_kernel, ts=ts, gc=gc, pos0=0),
        grid=(B, nst),
        in_specs=[pl.BlockSpec((ts, P), lambda b, s: (b * nst + s, 0)),
                  pl.BlockSpec((G, gc, gc), lambda b, s: (0, 0, 0)),
                  pl.BlockSpec((1, P), lambda b, s: (0, 0))],
        out_specs=pl.BlockSpec((ts, P), lambda b, s: (b * nst + s, 0)),
        out_shape=jax.ShapeDtypeStruct((B * S, P), BF16),
        scratch_shapes=[pltpu.VMEM((POOL_HIST + ts, P), F32)],
        compiler_params=_cparams(2),
        name="pool_prompt",
    )(u_all, wp_bf, pool_scale.reshape(1, P))


def _pool_sample_kernel(hist_ref, u_ref, wp_ref, ps_ref, a_ref, ext_ref, *, bb, sq, gc, pos0):
    ext_ref[:, 0:POOL_HIST, :] = hist_ref[...]
    ext_ref[:, POOL_HIST:, :] = u_ref[...]
    t_abs = pos0 + lax.broadcasted_iota(I32, (1, sq, 1), 1)
    for g, w in enumerate(POOL_WINDOWS):
        cols = slice(g * gc, (g + 1) * gc)
        cur = ext_ref[:, POOL_HIST:, cols]
        acc = cur
        for j in range(1, w):
            acc = acc + ext_ref[:, POOL_HIST - j:POOL_HIST - j + sq, cols]
        inv = 1.0 / jnp.minimum(t_abs + 1, w).astype(F32)
        pooled = (acc * inv - cur).reshape(bb * sq, gc)
        mixed = jnp.dot(pooled.astype(BF16), wp_ref[g], preferred_element_type=F32)
        a_ref[:, cols] = (mixed * ps_ref[:, cols]).astype(a_ref.dtype)


def _pool_sample(u_all, hist, wp_bf, pool_scale, row0, Bs, sq, bb=16):
    T, P = u_all.shape
    bb = _tile(Bs, bb)
    G = len(POOL_WINDOWS)
    gc = P // G
    u3 = u_all.reshape(T // sq, sq, P)
    blk0 = row0 // (sq * bb)
    return pl.pallas_call(
        functools.partial(_pool_sample_kernel, bb=bb, sq=sq, gc=gc, pos0=SAMPLE_POS0),
        grid=(Bs // bb,),
        in_specs=[pl.BlockSpec((bb, POOL_HIST, P), lambda i: (i, 0, 0)),
                  pl.BlockSpec((bb, sq, P), lambda i: (blk0 + i, 0, 0)),
                  pl.BlockSpec((G, gc, gc), lambda i: (0, 0, 0)),
                  pl.BlockSpec((1, P), lambda i: (0, 0))],
        out_specs=pl.BlockSpec((bb * sq, P), lambda i: (i, 0)),
        out_shape=jax.ShapeDtypeStruct((Bs * sq, P), BF16),
        scratch_shapes=[pltpu.VMEM((bb, POOL_HIST + sq, P), F32)],
        compiler_params=_cparams(1),
        name="pool_sample",
    )(hist, u3, wp_bf, pool_scale.reshape(1, P))


def _log_sigmoid(x):
    return jnp.minimum(x, 0.0) - jnp.log1p(jnp.exp(-jnp.abs(x)))


def _mlstm_cell(q, k, v, ig_c, ig_r, lf_c, lf_r, states, ls):
    L = q.shape[0]
    G = len(states)
    row = lax.broadcasted_iota(I32, (L, L), 0)
    col = lax.broadcasted_iota(I32, (L, L), 1)
    lower = col <= row
    upper = row <= col
    seg_c = None
    if G > 1:
        same = (row // ls) == (col // ls)
        lower = lower & same
        upper = upper & same
        seg_c = lax.broadcasted_iota(I32, (L, 1), 0) // ls

    def per_seq(vals):
        out = vals[0]
        for g in range(1, G):
            out = jnp.where(seg_c == g, vals[g], out)
        return out

    b_c = jnp.sum(jnp.where(lower, lf_r, 0.0), axis=1, keepdims=True)
    b_r = jnp.sum(jnp.where(upper, lf_c, 0.0), axis=0, keepdims=True)
    dmat = jnp.where(lower, b_c - b_r + ig_r, -jnp.inf)
    inter = b_c + per_seq([st[2] for st in states])
    m_new = jnp.maximum(inter, jnp.max(dmat, axis=1, keepdims=True))
    w_intra = jnp.exp(dmat - m_new)
    w_inter = jnp.exp(inter - m_new)
    s = lax.dot_general(q, k, (((1,), (1,)), ((), ())), preferred_element_type=F32) * w_intra
    qf = q.astype(F32)
    qc = [jnp.dot(q, st[0].astype(BF16), preferred_element_type=F32) for st in states]
    qn = [jnp.sum(qf * st[1], axis=1, keepdims=True) for st in states]
    if G > 1:
        q_c = qc[0]
        for g in range(1, G):
            q_c = jnp.where(seg_c == g, qc[g], q_c)
        q_n = per_seq(qn)
    else:
        q_c, q_n = qc[0], qn[0]
    num = jnp.dot(s.astype(BF16), v, preferred_element_type=F32) + w_inter * q_c
    den = jnp.sum(s, axis=1, keepdims=True) + w_inter * q_n
    h = num / jnp.maximum(jnp.abs(den), jnp.exp(-m_new))

    m_last = [m_new[(g + 1) * ls - 1:(g + 1) * ls, :] for g in range(G)]
    b_last = [b_c[(g + 1) * ls - 1:(g + 1) * ls, :] for g in range(G)]
    wl_c = jnp.exp(per_seq(b_last) - b_c + ig_c - per_seq(m_last))
    wk = k.astype(F32) * wl_c
    new_states = []
    for g, (C, n, m) in enumerate(states):
        wk_g = wk if G == 1 else jnp.where(seg_c == g, wk, 0.0)
        wl_inter = jnp.exp(b_last[g] + m - m_last[g])
        kv = lax.dot_general(wk_g.astype(BF16), v, (((0,), (0,)), ((), ())),
                             preferred_element_type=F32)
        new_states.append((wl_inter * C + kv,
                           wl_inter * n + jnp.sum(wk_g, axis=0, keepdims=True),
                           m_last[g]))
    return h, new_states


def _head_out(h, gh, o):
    return _rmsnorm(h, gh) * jax.nn.sigmoid(o.astype(F32))


def _gate_cols(g_c, g_r, bc_ref, br_ref):
    pre_c = g_c + br_ref[...]
    pre_r = g_r + bc_ref[...]
    return pre_c, _log_sigmoid(pre_c), pre_r, _log_sigmoid(pre_r)


def _mlstm_prompt_kernel(q_ref, k_ref, v_ref, o_ref, gc_ref, gr_ref, bc_ref, br_ref, gh_ref,
                         hb_ref, C_out, n_out, m_out, C_s, n_s, m_s, *, nh, d):
    c = pl.program_id(1)

    @pl.when(c == 0)
    def _():
        C_s[...] = jnp.zeros(C_s.shape, F32)
        n_s[...] = jnp.zeros(n_s.shape, F32)
        m_s[...] = jnp.zeros(m_s.shape, F32)

    L = q_ref.shape[0]
    pre_c, lf_call, pre_r, lf_rall = _gate_cols(gc_ref[...], gr_ref[...], bc_ref, br_ref)
    for hd in range(nh):
        cols = slice(hd * d, (hd + 1) * d)
        st = (C_s[hd], n_s[hd], m_s[hd])
        h, (new,) = _mlstm_cell(
            q_ref[:, cols], k_ref[:, cols], v_ref[:, cols],
            pre_c[:, hd:hd + 1], pre_r[hd:hd + 1, :],
            lf_call[:, nh + hd:nh + hd + 1], lf_rall[nh + hd:nh + hd + 1, :],
            [st], L)
        C_s[hd], n_s[hd], m_s[hd] = new
        hb_ref[:, cols] = _head_out(h, gh_ref[:, cols], o_ref[:, cols]).astype(hb_ref.dtype)

    @pl.when(c == pl.num_programs(1) - 1)
    def _():
        C_out[0] = C_s[...]
        n_out[0] = n_s[...]
        m_out[0] = m_s[...]


def _mlstm_prompt(qkvo, gates_c, gates_r, b_if, g_head, B, S, nh, chunk=256):
    W = qkvo.shape[1] // 4
    d = W // nh
    L = _tile(S, chunk)
    nc = S // L
    tok = lambda j: pl.BlockSpec((L, W), lambda b, c, j=j: (b * nc + c, j))
    const = lambda shape: pl.BlockSpec(shape, lambda b, c: (0,) * len(shape))
    return pl.pallas_call(
        functools.partial(_mlstm_prompt_kernel, nh=nh, d=d),
        grid=(B, nc),
        in_specs=[tok(0), tok(1), tok(2), tok(3),
                  pl.BlockSpec((L, 2 * nh), lambda b, c: (b * nc + c, 0)),
                  pl.BlockSpec((2 * nh, L), lambda b, c: (0, b * nc + c)),
                  const((2 * nh, 1)), const((1, 2 * nh)), const((1, W))],
        out_specs=[pl.BlockSpec((L, W), lambda b, c: (b * nc + c, 0)),
                   pl.BlockSpec((1, nh, d, d), lambda b, c: (b, 0, 0, 0)),
                   pl.BlockSpec((1, nh, 1, d), lambda b, c: (b, 0, 0, 0)),
                   pl.BlockSpec((1, nh, 1, 1), lambda b, c: (b, 0, 0, 0))],
        out_shape=[jax.ShapeDtypeStruct((B * S, W), BF16),
                   jax.ShapeDtypeStruct((B, nh, d, d), F32),
                   jax.ShapeDtypeStruct((B, nh, 1, d), F32),
                   jax.ShapeDtypeStruct((B, nh, 1, 1), F32)],
        scratch_shapes=[pltpu.VMEM((nh, d, d), F32), pltpu.VMEM((nh, 1, d), F32),
                        pltpu.VMEM((nh, 1, 1), F32)],
        compiler_params=_cparams(2),
        name="mlstm_prompt",
    )(qkvo, qkvo, qkvo, qkvo, gates_c, gates_r, b_if.reshape(2 * nh, 1), b_if.reshape(1, 2 * nh),
      g_head.reshape(1, W))


def _mlstm_sample_kernel(q_ref, k_ref, v_ref, o_ref, gc_ref, gr_ref, bc_ref, br_ref, gh_ref,
                         C_hbm, n_in, m_in, hb_ref, C_out, n_out, m_out, cbuf, csem,
                         *, nh, d, bb, sq):
    i = pl.program_id(0)
    n_steps = pl.num_programs(0)

    def c_copy(step, slot):
        return pltpu.make_async_copy(C_hbm.at[pl.ds(step * bb, bb)], cbuf.at[slot], csem.at[slot])

    @pl.when(i == 0)
    def _():
        c_copy(0, 0).start()

        @pl.when(n_steps > 1)
        def _():
            c_copy(1, 1).start()

    @pl.when(i + 2 < n_steps)
    def _():
        c_copy(i + 2, (i + 2) % 3).start()

    c_copy(i, i % 3).wait()
    C_in = cbuf.at[i % 3]
    pre_c, lf_call, pre_r, lf_rall = _gate_cols(gc_ref[...], gr_ref[0], bc_ref, br_ref)
    for hd in range(nh):
        cols = slice(hd * d, (hd + 1) * d)
        states = [(C_in[j, hd], n_in[j, hd], m_in[j, hd]) for j in range(bb)]
        h, new = _mlstm_cell(
            q_ref[:, cols], k_ref[:, cols], v_ref[:, cols],
            pre_c[:, hd:hd + 1], pre_r[hd:hd + 1, :],
            lf_call[:, nh + hd:nh + hd + 1], lf_rall[nh + hd:nh + hd + 1, :],
            states, sq)
        for j in range(bb):
            C_out[j, hd], n_out[j, hd], m_out[j, hd] = new[j]
        hb_ref[:, cols] = _head_out(h, gh_ref[:, cols], o_ref[:, cols]).astype(hb_ref.dtype)


def _mlstm_sample(qkvo, gates_c, gates_r, b_if, g_head, C0, n0, m0, row0, Bs, sq, nh, bb=2):
    W = qkvo.shape[1] // 4
    d = W // nh
    L = bb * sq
    blk0 = row0 // L
    tok = lambda j: pl.BlockSpec((L, W), lambda i, j=j: (blk0 + i, j))
    const = lambda shape: pl.BlockSpec(shape, lambda i: (0,) * len(shape))
    st = lambda a, b: pl.BlockSpec((bb, nh, a, b), lambda i: (i, 0, 0, 0))
    return pl.pallas_call(
        functools.partial(_mlstm_sample_kernel, nh=nh, d=d, bb=bb, sq=sq),
        grid=(Bs // bb,),
        in_specs=[tok(0), tok(1), tok(2), tok(3),
                  pl.BlockSpec((L, 2 * nh), lambda i: (blk0 + i, 0)),
                  pl.BlockSpec((1, 2 * nh, L), lambda i: (i, 0, 0)),
                  const((2 * nh, 1)), const((1, 2 * nh)), const((1, W)),
                  pl.BlockSpec(memory_space=pl.ANY), st(1, d), st(1, 1)],
        out_specs=[pl.BlockSpec((L, W), lambda i: (i, 0)), st(d, d), st(1, d), st(1, 1)],
        out_shape=[jax.ShapeDtypeStruct((Bs * sq, W), BF16),
                   jax.ShapeDtypeStruct((Bs, nh, d, d), F32),
                   jax.ShapeDtypeStruct((Bs, nh, 1, d), F32),
                   jax.ShapeDtypeStruct((Bs, nh, 1, 1), F32)],
        scratch_shapes=[pltpu.VMEM((3, bb, nh, d, d), F32), pltpu.SemaphoreType.DMA((3,))],
        compiler_params=_cparams(1),
        name="mlstm_sample",
    )(qkvo, qkvo, qkvo, qkvo, gates_c, gates_r, b_if.reshape(2 * nh, 1), b_if.reshape(1, 2 * nh),
      g_head.reshape(1, W), C0, n0.reshape(Bs, nh, 1, d), m0.reshape(Bs, nh, 1, 1))


def _merge_kernel(ap_ref, as_ref, hp_ref, hs_ref, ga_ref, gb_ref, wa_ref, wb_ref, o_ref,
                  wa_bf, wb_bf, *, n_first):
    m = pl.program_id(1)

    @pl.when(m == 0)
    def _():
        wa_bf[...] = wa_ref[...].astype(BF16)
        wb_bf[...] = wb_ref[...].astype(BF16)

    def body(a, hb):
        pa = jnp.dot(a, wa_bf[...], preferred_element_type=F32)
        pb = jnp.dot(hb, wb_bf[...], preferred_element_type=F32)
        mix = (jax.nn.sigmoid(ga_ref[...].astype(F32)) * pa
               + jax.nn.sigmoid(gb_ref[...].astype(F32)) * pb)
        o_ref[...] = mix.astype(o_ref.dtype)

    @pl.when(m < n_first)
    def _():
        body(ap_ref[...], hp_ref[...])

    @pl.when(m >= n_first)
    def _():
        body(as_ref[...], hs_ref[...])


def _merge(a_p, a_s, hb_p, hb_s, gab, wa, wb, tm=512, tn=1024):
    Tp, P = a_p.shape
    T = Tp + a_s.shape[0]
    W = hb_p.shape[1]
    D = wa.shape[1]
    tm = _tile(math.gcd(Tp, T - Tp), tm)
    tn = _tile(D, tn)
    nn = D // tn
    n_first = Tp // tm
    a_specs = _split_specs((tm, P), n_first, lambda n, m: m)
    h_specs = _split_specs((tm, W), n_first, lambda n, m: m)
    return pl.pallas_call(
        functools.partial(_merge_kernel, n_first=n_first),
        grid=(nn, T // tm),
        in_specs=[*a_specs, *h_specs,
                  pl.BlockSpec((tm, tn), lambda n, m: (m, n)),
                  pl.BlockSpec((tm, tn), lambda n, m: (m, nn + n)),
                  pl.BlockSpec((P, tn), lambda n, m: (0, n)),
                  pl.BlockSpec((W, tn), lambda n, m: (0, n))],
        out_specs=pl.BlockSpec((tm, tn), lambda n, m: (m, n)),
        out_shape=jax.ShapeDtypeStruct((T, D), BF16),
        scratch_shapes=[pltpu.VMEM((P, tn), BF16), pltpu.VMEM((W, tn), BF16)],
        compiler_params=_cparams(2),
        name="merge",
    )(a_p, a_s, hb_p, hb_s, gab, gab, wa, wb)


def _moe_local_rows(tt):
    return -(-(2 * tt + N_EXPERTS * (SUBLANES - 1)) // LANES) * LANES


def _router_kernel(x_ref, g_ref, w_ref, b_ref, xl_ref, meta_ref, cnt_ref, *, lr):
    tt = x_ref.shape[0]
    xn = _rmsnorm(x_ref[...], g_ref[...])
    logits = _dot_split(xn, w_ref[...]) + b_ref[...]
    lane = lax.broadcasted_iota(I32, logits.shape, 1).astype(F32)
    neg = -jnp.inf
    gl = jnp.where(lane < N_EXPERT_GROUPS, logits, neg)
    gmax = jnp.max(gl, axis=1, keepdims=True)
    g_idx = jnp.min(jnp.where(gl == gmax, lane, float(LANES)), axis=1, keepdims=True)
    g_val = 1.0 / jnp.sum(jnp.exp(gl - gmax), axis=1, keepdims=True)
    lo = N_EXPERT_GROUPS + g_idx * EXPERTS_PER_GROUP
    el = jnp.where((lane >= lo) & (lane < lo + EXPERTS_PER_GROUP), logits, neg)
    t1 = jnp.max(el, axis=1, keepdims=True)
    i1 = jnp.min(jnp.where(el == t1, lane, float(LANES)), axis=1, keepdims=True)
    el2 = jnp.where(lane == i1, neg, el)
    t2 = jnp.max(el2, axis=1, keepdims=True)
    i2 = jnp.min(jnp.where(el2 == t2, lane, float(LANES)), axis=1, keepdims=True)
    e2 = jnp.exp(t2 - t1)
    w1 = g_val / (1.0 + e2)
    w2 = g_val * e2 / (1.0 + e2)

    oh1 = (lane == i1 - N_EXPERT_GROUPS).astype(F32)
    oh2 = (lane == i2 - N_EXPERT_GROUPS).astype(F32)
    r_i = lax.broadcasted_iota(I32, (tt, tt), 0)
    c_i = lax.broadcasted_iota(I32, (tt, tt), 1)
    before = (c_i < r_i).astype(BF16)
    rank1 = jnp.dot(before, oh1.astype(BF16), preferred_element_type=F32)
    rank2 = jnp.dot(before, oh2.astype(BF16), preferred_element_type=F32)
    cnt1 = jnp.sum(oh1, axis=0, keepdims=True)
    cnt = cnt1 + jnp.sum(oh2, axis=0, keepdims=True)
    cnt8 = jnp.floor((cnt + (SUBLANES - 1)) * (1.0 / SUBLANES)) * SUBLANES
    e_r = lax.broadcasted_iota(I32, (LANES, LANES), 0)
    e_c = lax.broadcasted_iota(I32, (LANES, LANES), 1)
    start = jnp.dot(jnp.broadcast_to(cnt8, (2 * SUBLANES, LANES)).astype(BF16),
                    (e_r < e_c).astype(BF16), preferred_element_type=F32)[0:1]
    row1 = jnp.sum(oh1 * (start + rank1), axis=1, keepdims=True)
    row2 = jnp.sum(oh2 * (start + cnt1 + rank2), axis=1, keepdims=True)
    meta = jnp.where(lane == 0.0, row1, jnp.where(lane == 1.0, row2,
                     jnp.where(lane == 2.0, w1, jnp.where(lane == 3.0, w2, 0.0))))
    meta_ref[...] = meta
    cnt_ref[0] = cnt8.astype(I32)

    meta_t = meta.T
    dst = lax.broadcasted_iota(I32, (lr, tt), 0).astype(F32)
    place = ((dst == meta_t[0:1, :]) | (dst == meta_t[1:2, :])).astype(BF16)
    xl_ref[...] = jnp.dot(place, xn.astype(BF16), preferred_element_type=F32)


def _router(x1, g_ffn, w_r, b_r, tt):
    T, D = x1.shape
    lr = _moe_local_rows(tt)
    nj = T // tt
    return pl.pallas_call(
        functools.partial(_router_kernel, lr=lr),
        grid=(nj,),
        in_specs=[pl.BlockSpec((tt, D), lambda i: (i, 0)),
                  pl.BlockSpec((1, D), lambda i: (0, 0)),
                  pl.BlockSpec((D, 2 * LANES), lambda i: (0, 0)),
                  pl.BlockSpec((1, LANES), lambda i: (0, 0))],
        out_specs=[pl.BlockSpec((lr, D), lambda i: (i, 0)),
                   pl.BlockSpec((tt, LANES), lambda i: (i, 0)),
                   pl.BlockSpec((1, 1, LANES), lambda i: (i, 0, 0))],
        out_shape=[jax.ShapeDtypeStruct((nj * lr, D), F32),
                   jax.ShapeDtypeStruct((T, LANES), F32),
                   jax.ShapeDtypeStruct((nj, 1, LANES), I32)],
        compiler_params=_cparams(1),
        name="moe_route",
    )(x1, g_ffn.reshape(1, D), w_r, b_r)


def _moe_plan(cnt, tt, lr, tr):
    nj, E = cnt.shape
    n_tiles = nj * (2 * tt + E * (SUBLANES - 1)) // tr + E
    pre = jnp.cumsum(cnt, axis=0) - cnt
    tot = jnp.sum(cnt, axis=0)
    lstart = jnp.cumsum(cnt, axis=1) - cnt
    nt_e = (tot + tr - 1) // tr
    t_end = jnp.cumsum(nt_e)
    t_start = t_end - nt_e
    tile = jnp.arange(n_tiles, dtype=I32)
    tile_e = jnp.minimum(jnp.sum((t_end[None, :] <= tile[:, None]).astype(I32), axis=1), E - 1)
    tile_a = (tile - t_start[tile_e]) * tr
    tile_next = t_end[tile_e]
    n_used = t_end[-1].reshape(1)
    src_local = jnp.arange(nj, dtype=I32)[:, None] * lr + lstart
    src_global = t_start[None, :] * tr + pre
    as_i32 = lambda a: a.astype(I32)
    by_expert = lambda a: as_i32(a.T.reshape(-1))
    by_tile = lambda a: as_i32(a.reshape(-1))
    return dict(tile_e=as_i32(tile_e), tile_a=as_i32(tile_a), tile_next=as_i32(tile_next),
                n_used=as_i32(n_used),
                pre_e=by_expert(pre), cnt_e=by_expert(cnt), src_e=by_expert(src_local),
                cnt_j=by_tile(cnt), dst_j=by_tile(lstart), src_j=by_tile(src_global),
                n_tiles=n_tiles)


def _experts_kernel(te_ref, ta_ref, tn_ref, nu_ref, pre_ref, cnt_ref, src_ref,
                    xl_hbm, wg_hbm, wu_hbm, wd_hbm, o_ref,
                    xbuf, wg_st, wu_st, wd_st, wg_bf, wu_bf, wd_bf, sem, wsem, *, nj, tr):
    i = pl.program_id(0)
    n_used = nu_ref[0]

    def weight_copies(e):
        return (pltpu.make_async_copy(wg_hbm.at[e], wg_st, wsem.at[0]),
                pltpu.make_async_copy(wu_hbm.at[e], wu_st, wsem.at[1]),
                pltpu.make_async_copy(wd_hbm.at[e], wd_st, wsem.at[2]))

    def for_each_run(tile, slot, fn):
        e = te_ref[tile]
        a = ta_ref[tile]

        def body(j, c):
            lo = pre_ref[e * nj + j]
            first = jnp.maximum(lo, a)
            n = jnp.minimum(lo + cnt_ref[e * nj + j], a + tr) - first

            @pl.when(n > 0)
            def _():
                src = pl.multiple_of(src_ref[e * nj + j] + (first - lo), SUBLANES)
                dst = pl.multiple_of(first - a, SUBLANES)
                rows = pl.multiple_of(n, SUBLANES)
                fn(pltpu.make_async_copy(xl_hbm.at[pl.ds(src, rows)],
                                         xbuf.at[slot, pl.ds(dst, rows)], sem.at[slot]))
            return c

        lax.fori_loop(0, nj, body, 0)

    @pl.when((i == 0) & (n_used > 0))
    def _():
        xbuf[...] = jnp.zeros(xbuf.shape, xbuf.dtype)
        for_each_run(0, 0, lambda cp: cp.start())
        for cp in weight_copies(te_ref[0]):
            cp.start()

    @pl.when(i + 1 < n_used)
    def _():
        for_each_run(i + 1, (i + 1) % 2, lambda cp: cp.start())

    @pl.when(i < n_used)
    def _():
        @pl.when((i == 0) | (te_ref[i] != te_ref[jnp.maximum(i - 1, 0)]))
        def _():
            for cp in weight_copies(te_ref[i]):
                cp.wait()
            wg_bf[...] = wg_st[...].astype(BF16)
            wu_bf[...] = wu_st[...].astype(BF16)
            wd_bf[...] = wd_st[...].astype(BF16)
            nxt = tn_ref[i]

            @pl.when(nxt < n_used)
            def _():
                for cp in weight_copies(te_ref[nxt]):
                    cp.start()

        for_each_run(i, i % 2, lambda cp: cp.wait())
        x = xbuf[i % 2].astype(BF16)
        hg = jnp.dot(x, wg_bf[...], preferred_element_type=F32)
        hu = jnp.dot(x, wu_bf[...], preferred_element_type=F32)
        h = hg * jax.nn.sigmoid(hg) * hu
        o_ref[...] = jnp.dot(h.astype(BF16), wd_bf[...], preferred_element_type=F32)

    @pl.when(i >= n_used)
    def _():
        o_ref[...] = jnp.zeros(o_ref.shape, o_ref.dtype)


def _experts(xl, plan, w_eg, w_eu, w_ed, nj, tr):
    D = xl.shape[1]
    E, _, FF = w_eg.shape
    n_tiles = plan["n_tiles"]
    any_space = pl.BlockSpec(memory_space=pl.ANY)
    return pl.pallas_call(
        functools.partial(_experts_kernel, nj=nj, tr=tr),
        grid_spec=pltpu.PrefetchScalarGridSpec(
            num_scalar_prefetch=7,
            grid=(n_tiles,),
            in_specs=[any_space, any_space, any_space, any_space],
            out_specs=pl.BlockSpec((tr, D), lambda i, *_: (i, 0)),
            scratch_shapes=[pltpu.VMEM((2, tr, D), F32),
                            pltpu.VMEM((D, FF), F32), pltpu.VMEM((D, FF), F32),
                            pltpu.VMEM((FF, D), F32),
                            pltpu.VMEM((D, FF), BF16), pltpu.VMEM((D, FF), BF16),
                            pltpu.VMEM((FF, D), BF16),
                            pltpu.SemaphoreType.DMA((2,)), pltpu.SemaphoreType.DMA((3,))]),
        out_shape=jax.ShapeDtypeStruct((n_tiles * tr, D), F32),
        compiler_params=_cparams(1),
        name="moe_experts",
    )(plan["tile_e"], plan["tile_a"], plan["tile_next"], plan["n_used"], plan["pre_e"],
      plan["cnt_e"], plan["src_e"], xl, w_eg, w_eu, w_ed)


def _combine_kernel(cnt_ref, dst_ref, src_ref, x1_ref, meta_ref, g_ref, ys_hbm, yp_ref, ys_ref,
                    ybuf, sem, *, n_first):
    j = pl.program_id(0)
    nj = pl.num_programs(0)
    lr = ybuf.shape[1]

    def for_each_run(tile, slot, fn):
        def body(e, c):
            n = cnt_ref[tile * N_EXPERTS + e]

            @pl.when(n > 0)
            def _():
                src = pl.multiple_of(src_ref[tile * N_EXPERTS + e], SUBLANES)
                dst = pl.multiple_of(dst_ref[tile * N_EXPERTS + e], SUBLANES)
                rows = pl.multiple_of(n, SUBLANES)
                fn(pltpu.make_async_copy(ys_hbm.at[pl.ds(src, rows)],
                                         ybuf.at[slot, pl.ds(dst, rows)], sem.at[slot]))
            return c

        lax.fori_loop(0, N_EXPERTS, body, 0)

    @pl.when(j == 0)
    def _():
        ybuf[...] = jnp.zeros(ybuf.shape, ybuf.dtype)
        for_each_run(0, 0, lambda cp: cp.start())

    @pl.when(j + 1 < nj)
    def _():
        for_each_run(j + 1, (j + 1) % 2, lambda cp: cp.start())

    for_each_run(j, j % 2, lambda cp: cp.wait())
    meta = meta_ref[...]
    col = lax.broadcasted_iota(I32, (meta.shape[0], lr), 1).astype(F32)
    pick = jnp.where(col == meta[:, 0:1], meta[:, 2:3],
                     jnp.where(col == meta[:, 1:2], meta[:, 3:4], 0.0))
    moe = jnp.dot(pick.astype(BF16), ybuf[j % 2].astype(BF16), preferred_element_type=F32)
    y = _rmsnorm(x1_ref[...] + moe, g_ref[...])

    @pl.when(j < n_first)
    def _():
        yp_ref[...] = y

    @pl.when(j >= n_first)
    def _():
        ys_ref[...] = y


def _combine(x1, meta, g_final, ys, plan, Tp, tt, lr):
    T, D = x1.shape
    nj = T // tt
    n_first = Tp // tt
    out_p, out_s = _split_specs((tt, D), n_first, lambda j, *_: j)
    return pl.pallas_call(
        functools.partial(_combine_kernel, n_first=n_first),
        grid_spec=pltpu.PrefetchScalarGridSpec(
            num_scalar_prefetch=3,
            grid=(nj,),
            in_specs=[pl.BlockSpec((tt, D), lambda j, *_: (j, 0)),
                      pl.BlockSpec((tt, LANES), lambda j, *_: (j, 0)),
                      pl.BlockSpec((1, D), lambda j, *_: (0, 0)),
                      pl.BlockSpec(memory_space=pl.ANY)],
            out_specs=[out_p, out_s],
            scratch_shapes=[pltpu.VMEM((2, lr, D), F32), pltpu.SemaphoreType.DMA((2,))]),
        out_shape=[jax.ShapeDtypeStruct((Tp, D), F32), jax.ShapeDtypeStruct((T - Tp, D), F32)],
        compiler_params=_cparams(1),
        name="moe_combine",
    )(plan["cnt_j"], plan["dst_j"], plan["src_j"], x1, meta, g_final.reshape(1, D), ys)


def _pad_cols(w, n):
    return jnp.pad(w, ((0, 0), (0, n - w.shape[1])))


def _layer(xp, xs, Bp, Sp, Bs, Ss, state_pool, state_C, state_n, state_m, g_mix, w_in, b_if,
           w_pool, pool_scale, w_proj_a, w_proj_b, g_head, w_out, g_ffn, w_rg, b_rg, w_re, b_re,
           w_eg, w_eu, w_ed, g_out):
    D = xp.shape[1]
    Tp = Bp * Sp
    T = Tp + Bs * Ss
    P = w_pool.shape[0] * w_pool.shape[1]
    nh = b_if.shape[0] // 2
    W = w_proj_b.shape[0]
    d = W // nh
    gate0 = P + 4 * W

    w_in_t = w_in.T
    xn, gates = _rms_proj(xp, xs, g_mix, w_in_t[gate0:gate0 + 2 * nh].T)
    gates = gates[:, :2 * nh]
    u = _matmul_wt(xn, w_in_t, 0, P, F32)
    k_scale = jnp.concatenate([jnp.ones((W,), F32), jnp.full((W,), d ** -0.5, F32),
                               jnp.ones((2 * W,), F32)])
    qkvo = _matmul_wt(xn, w_in_t, P, 4 * W, BF16, scale=k_scale)
    gab = _matmul_wt(xn, w_in_t, gate0 + 2 * nh, 2 * D, BF16)

    wp_bf = w_pool.astype(BF16)
    nbuf = state_pool.shape[1]
    hist = jnp.pad(state_pool, ((0, 0), (POOL_HIST - nbuf, 0), (0, 0)))
    a_p = _pool_prompt(u, wp_bf, pool_scale, Bp, Sp)
    a_s = _pool_sample(u, hist, wp_bf, pool_scale, Tp, Bs, Ss)
    pool_p = jnp.stack([u[(b + 1) * Sp - nbuf:(b + 1) * Sp] for b in range(Bp)])
    pool_s = jnp.concatenate([state_pool, u[Tp:].reshape(Bs, Ss, P)], axis=1)[:, -nbuf:]

    gates_r = gates.T
    hb_p, C_p, n_p, m_p = _mlstm_prompt(qkvo, gates, gates_r, b_if, g_head, Bp, Sp, nh)
    bb = 2
    gr_s = gates_r[:, Tp:].reshape(2 * nh, Bs // bb, bb * Ss).transpose(1, 0, 2)
    hb_s, C_s, n_s, m_s = _mlstm_sample(qkvo, gates, gr_s, b_if, g_head, state_C, state_n,
                                        state_m, Tp, Bs, Ss, nh, bb=bb)

    mix = _merge(a_p, a_s, hb_p, hb_s, gab, w_proj_a, w_proj_b)
    x1 = _matmul(mix, w_out, F32, res=(xp, xs))

    tt = _tile(math.gcd(Tp, T - Tp), MOE_TT)
    lr = _moe_local_rows(tt)
    w_r = _split_weight(jnp.concatenate([w_rg, w_re], axis=1))
    b_r = _pad_cols(jnp.concatenate([b_rg, b_re]).reshape(1, -1), LANES)
    xl, meta, cnt = _router(x1, g_ffn, w_r, b_r, tt)
    plan = _moe_plan(cnt[:, 0, :N_EXPERTS], tt, lr, MOE_TR)
    ys = _experts(xl, plan, w_eg, w_eu, w_ed, T // tt, MOE_TR)
    y_p, y_s = _combine(x1, meta, g_out, ys, plan, Tp, tt, lr)
    states_p = (pool_p, C_p, n_p.reshape(Bp, nh, d), m_p.reshape(Bp, nh))
    states_s = (pool_s, C_s, n_s.reshape(Bs, nh, d), m_s.reshape(Bs, nh))
    return y_p, y_s, states_p, states_s


def kernel(x_prompt, x_sample, state_pool, state_C, state_n, state_m, g_mix, w_in, b_if, w_pool, pool_scale, w_proj_a, w_proj_b, g_head, w_out, g_ffn, w_router_group, b_router_group, w_router_expert, b_router_expert, w_exp_gate, w_exp_up, w_exp_down, g_final):
    Bp, Sp, D = x_prompt.shape
    Bs, Ss, _ = x_sample.shape
    assert g_mix.shape[0] == 1, "the closing norm is fused into the layer's last kernel"
    l = 0
    y_p, y_s, sp, ss = _layer(
        x_prompt.reshape(Bp * Sp, D), x_sample.reshape(Bs * Ss, D), Bp, Sp, Bs, Ss,
        state_pool[l], state_C[l], state_n[l], state_m[l], g_mix[l], w_in[l], b_if[l], w_pool[l],
        pool_scale[l], w_proj_a[l], w_proj_b[l], g_head[l], w_out[l], g_ffn[l],
        w_router_group[l], b_router_group[l], w_router_expert[l], b_router_expert[l],
        w_exp_gate[l], w_exp_up[l], w_exp_down[l], g_final)
    return (y_p.reshape(Bp, Sp, D), y_s.reshape(Bs, Ss, D),
            sp[0][None], sp[1][None], sp[2][None], sp[3][None],
            ss[0][None], ss[1][None], ss[2][None], ss[3][None])
```

```python
import functools
import math

import jax
import jax.numpy as jnp
from jax import lax
from jax.experimental import pallas as pl
from jax.experimental.pallas import tpu as pltpu

F32 = jnp.float32
BF16 = jnp.bfloat16
I32 = jnp.int32

RMS_EPS = 1e-6
POOL_WINDOWS = (2, 4, 8, 16)
POOL_HIST = 16
SAMPLE_POS0 = 16384
N_EXPERT_GROUPS = 4
EXPERTS_PER_GROUP = 8
N_EXPERTS = N_EXPERT_GROUPS * EXPERTS_PER_GROUP
LANES = 128
SUBLANES = 8
VMEM_LIMIT = 56 * 1024 * 1024
MOE_TT = 256
MOE_TR = 256
WEIGHT_DMA_PRIORITY = 1


def _cparams(n_axes):
    return pltpu.CompilerParams(dimension_semantics=("arbitrary",) * n_axes,
                                vmem_limit_bytes=VMEM_LIMIT)


def _tile(n, pref):
    t = min(n, pref)
    while n % t:
        t -= 1
    return t


def _split_specs(shape, n_first, axis_fn):
    first = pl.BlockSpec(shape, lambda *g: (jnp.minimum(axis_fn(*g), n_first - 1), 0))
    second = pl.BlockSpec(shape, lambda *g: (jnp.maximum(axis_fn(*g) - n_first, 0), 0))
    return first, second


def _rmsnorm(x, g):
    return x * lax.rsqrt(jnp.mean(x * x, axis=-1, keepdims=True) + RMS_EPS) * g


def _split_weight(w):
    w = _pad_cols(w, LANES)
    hi = w.astype(BF16)
    lo = (w - hi.astype(F32)).astype(BF16)
    return jnp.concatenate([hi, lo], axis=1)


def _dot_split(x, w_hl):
    xh = x.astype(BF16)
    xl = (x - xh.astype(F32)).astype(BF16)
    p = jnp.dot(xh, w_hl, preferred_element_type=F32)
    return (p[:, :LANES] + p[:, LANES:]
            + jnp.dot(xl, w_hl[:, :LANES], preferred_element_type=F32))


def _rms_proj_kernel(xp_ref, xs_ref, g_ref, w_ref, xn_ref, p_ref, *, n_first):
    def body(x):
        xn = _rmsnorm(x, g_ref[...])
        xn_ref[...] = xn.astype(xn_ref.dtype)
        p_ref[...] = _dot_split(xn, w_ref[...])

    i = pl.program_id(0)

    @pl.when(i < n_first)
    def _():
        body(xp_ref[...])

    @pl.when(i >= n_first)
    def _():
        body(xs_ref[...])


def _rms_proj(xp, xs, g, w_small, tm=512):
    Tp, D = xp.shape
    Ts = xs.shape[0]
    tm = _tile(math.gcd(Tp, Ts), tm)
    T = Tp + Ts
    sp, ss = _split_specs((tm, D), Tp // tm, lambda i: i)
    return pl.pallas_call(
        functools.partial(_rms_proj_kernel, n_first=Tp // tm),
        grid=(T // tm,),
        in_specs=[sp, ss,
                  pl.BlockSpec((1, D), lambda i: (0, 0)),
                  pl.BlockSpec((D, 2 * LANES), lambda i: (0, 0))],
        out_specs=[pl.BlockSpec((tm, D), lambda i: (i, 0)),
                   pl.BlockSpec((tm, LANES), lambda i: (i, 0))],
        out_shape=[jax.ShapeDtypeStruct((T, D), BF16),
                   jax.ShapeDtypeStruct((T, LANES), F32)],
        compiler_params=_cparams(1),
        name="rms_proj",
    )(xp, xs, g.reshape(1, D), _split_weight(w_small))


def _mm_kernel(*refs, n_res_first):
    x_ref, w_ref = refs[0], refs[1]
    res_refs = refs[2:4] if n_res_first is not None else ()
    o_ref, wbf_ref = refs[-2:]

    @pl.when(pl.program_id(1) == 0)
    def _():
        wbf_ref[...] = w_ref[...].astype(BF16)

    acc = jnp.dot(x_ref[...], wbf_ref[...], preferred_element_type=F32)
    if n_res_first is None:
        o_ref[...] = acc.astype(o_ref.dtype)
    else:
        m = pl.program_id(1)

        @pl.when(m < n_res_first)
        def _():
            o_ref[...] = (acc + res_refs[0][...]).astype(o_ref.dtype)

        @pl.when(m >= n_res_first)
        def _():
            o_ref[...] = (acc + res_refs[1][...]).astype(o_ref.dtype)


def _matmul(x, w, out_dtype, res=None, tm=512, tn=1024):
    T, K = x.shape
    ncols = w.shape[1]
    tn = _tile(ncols, tn)
    if res is not None:
        tm = _tile(math.gcd(res[0].shape[0], res[1].shape[0]), tm)
    tm = _tile(T, tm)
    in_specs = [pl.BlockSpec((tm, K), lambda n, m: (m, 0)),
                pl.BlockSpec((K, tn), lambda n, m: (0, n))]
    args = [x, w]
    n_res_first = None
    if res is not None:
        n_res_first = res[0].shape[0] // tm
        first = pl.BlockSpec((tm, tn), lambda n, m: (jnp.minimum(m, n_res_first - 1), n))
        second = pl.BlockSpec((tm, tn), lambda n, m: (jnp.maximum(m - n_res_first, 0), n))
        in_specs += [first, second]
        args += list(res)
    return pl.pallas_call(
        functools.partial(_mm_kernel, n_res_first=n_res_first),
        grid=(ncols // tn, T // tm),
        in_specs=in_specs,
        out_specs=pl.BlockSpec((tm, tn), lambda n, m: (m, n)),
        out_shape=jax.ShapeDtypeStruct((T, ncols), out_dtype),
        scratch_shapes=[pltpu.VMEM((K, tn), BF16)],
        compiler_params=_cparams(2),
        name="mm_wstat",
    )(*args)


def _mm_wt_kernel(*refs, has_scale, row0, tn):
    x_ref, wt_hbm = refs[0], refs[1]
    scale_ref = refs[2] if has_scale else None
    o_ref, stage, wbf_ref, sem = refs[-4:]
    n = pl.program_id(0)
    nn = pl.num_programs(0)

    def w_copy(tile, slot):
        start = pl.multiple_of(row0 + tile * tn, SUBLANES)
        return pltpu.make_async_copy(wt_hbm.at[pl.ds(start, tn)], stage.at[slot], sem.at[slot])

    @pl.when(pl.program_id(1) == 0)
    def _():
        @pl.when(n == 0)
        def _():
            w_copy(0, 0).start()

        w_copy(n, n % 2).wait()

        @pl.when(n + 1 < nn)
        def _():
            w_copy(n + 1, (n + 1) % 2).start()

        wbf_ref[...] = stage[n % 2].T.astype(BF16)

    acc = jnp.dot(x_ref[...], wbf_ref[...], preferred_element_type=F32)
    if has_scale:
        acc = acc * scale_ref[...]
    o_ref[...] = acc.astype(o_ref.dtype)


def _matmul_wt(x, w_t, row0, ncols, out_dtype, scale=None, tm=1024, tn=1024):
    T, K = x.shape
    assert row0 % SUBLANES == 0
    tm = _tile(T, tm)
    tn = _tile(ncols, tn)
    in_specs = [pl.BlockSpec((tm, K), lambda n, m: (m, 0)),
                pl.BlockSpec(memory_space=pl.ANY)]
    args = [x, w_t]
    if scale is not None:
        in_specs.append(pl.BlockSpec((1, tn), lambda n, m: (0, n)))
        args.append(scale.reshape(1, ncols))
    return pl.pallas_call(
        functools.partial(_mm_wt_kernel, has_scale=scale is not None, row0=row0, tn=tn),
        grid=(ncols // tn, T // tm),
        in_specs=in_specs,
        out_specs=pl.BlockSpec((tm, tn), lambda n, m: (m, n)),
        out_shape=jax.ShapeDtypeStruct((T, ncols), out_dtype),
        scratch_shapes=[pltpu.VMEM((2, tn, K), F32), pltpu.VMEM((K, tn), BF16),
                        pltpu.SemaphoreType.DMA((2,))],
        compiler_params=_cparams(2),
        name="mm_wt",
    )(*args)


def _pool_prompt_kernel(u_ref, wp_ref, ps_ref, a_ref, ext_ref, *, ts, gc, pos0):
    s = pl.program_id(1)

    @pl.when(s == 0)
    def _():
        ext_ref[0:POOL_HIST, :] = jnp.zeros((POOL_HIST, ext_ref.shape[1]), F32)

    ext_ref[POOL_HIST:, :] = u_ref[...]
    t_abs = pos0 + s * ts + lax.broadcasted_iota(I32, (ts, 1), 0)
    for g, w in enumerate(POOL_WINDOWS):
        cols = slice(g * gc, (g + 1) * gc)
        e = ext_ref[:, cols]
        acc = e
        span = 1
        while span < w:
            acc = acc + pltpu.roll(acc, span, 0)
            span *= 2
        inv = 1.0 / jnp.minimum(t_abs + 1, w).astype(F32)
        pooled = acc[POOL_HIST:, :] * inv - e[POOL_HIST:, :]
        mixed = jnp.dot(pooled.astype(BF16), wp_ref[g], preferred_element_type=F32)
        a_ref[:, cols] = (mixed * ps_ref[:, cols]).astype(a_ref.dtype)
    ext_ref[0:POOL_HIST, :] = ext_ref[ts:ts + POOL_HIST, :]


def _pool_prompt(u_all, wp_bf, pool_scale, B, S, ts=512):
    P = u_all.shape[1]
    ts = _tile(S, ts)
    nst = S // ts
    G = len(POOL_WINDOWS)
    gc = P // G
    return pl.pallas_call(
        functools.partial(_pool_prompt_kernel, ts=ts, gc=gc, pos0=0),
        grid=(B, nst),
        in_specs=[pl.BlockSpec((ts, P), lambda b, s: (b * nst + s, 0)),
                  pl.BlockSpec((G, gc, gc), lambda b, s: (0, 0, 0)),
                  pl.BlockSpec((1, P), lambda b, s: (0, 0))],
        out_specs=pl.BlockSpec((ts, P), lambda b, s: (b * nst + s, 0)),
        out_shape=jax.ShapeDtypeStruct((B * S, P), BF16),
        scratch_shapes=[pltpu.VMEM((POOL_HIST + ts, P), F32)],
        compiler_params=_cparams(2),
        name="pool_prompt",
    )(u_all, wp_bf, pool_scale.reshape(1, P))


def _pool_sample_kernel(hist_ref, u_ref, wp_ref, ps_ref, a_ref, ext_ref, *, bb, sq, gc, pos0):
    ext_ref[:, 0:POOL_HIST, :] = hist_ref[...]
    ext_ref[:, POOL_HIST:, :] = u_ref[...]
    t_abs = pos0 + lax.broadcasted_iota(I32, (1, sq, 1), 1)
    for g, w in enumerate(POOL_WINDOWS):
        cols = slice(g * gc, (g + 1) * gc)
        cur = ext_ref[:, POOL_HIST:, cols]
        acc = cur
        for j in range(1, w):
            acc = acc + ext_ref[:, POOL_HIST - j:POOL_HIST - j + sq, cols]
        inv = 1.0 / jnp.minimum(t_abs + 1, w).astype(F32)
        pooled = (acc * inv - cur).reshape(bb * sq, gc)
        mixed = jnp.dot(pooled.astype(BF16), wp_ref[g], preferred_element_type=F32)
        a_ref[:, cols] = (mixed * ps_ref[:, cols]).astype(a_ref.dtype)


def _pool_sample(u_all, hist, wp_bf, pool_scale, row0, Bs, sq, bb=16):
    T, P = u_all.shape
    bb = _tile(Bs, bb)
    G = len(POOL_WINDOWS)
    gc = P // G
    u3 = u_all.reshape(T // sq, sq, P)
    blk0 = row0 // (sq * bb)
    return pl.pallas_call(
        functools.partial(_pool_sample_kernel, bb=bb, sq=sq, gc=gc, pos0=SAMPLE_POS0),
        grid=(Bs // bb,),
        in_specs=[pl.BlockSpec((bb, POOL_HIST, P), lambda i: (i, 0, 0)),
                  pl.BlockSpec((bb, sq, P), lambda i: (blk0 + i, 0, 0)),
                  pl.BlockSpec((G, gc, gc), lambda i: (0, 0, 0)),
                  pl.BlockSpec((1, P), lambda i: (0, 0))],
        out_specs=pl.BlockSpec((bb * sq, P), lambda i: (i, 0)),
        out_shape=jax.ShapeDtypeStruct((Bs * sq, P), BF16),
        scratch_shapes=[pltpu.VMEM((bb, POOL_HIST + sq, P), F32)],
        compiler_params=_cparams(1),
        name="pool_sample",
    )(hist, u3, wp_bf, pool_scale.reshape(1, P))


def _log_sigmoid(x):
    return jnp.minimum(x, 0.0) - jnp.log1p(jnp.exp(-jnp.abs(x)))


def _mlstm_cell(q, k, v, ig_c, ig_r, lf_c, lf_r, states, ls):
    L = q.shape[0]
    G = len(states)
    row = lax.broadcasted_iota(I32, (L, L), 0)
    col = lax.broadcasted_iota(I32, (L, L), 1)
    lower = col <= row
    upper = row <= col
    seg_c = None
    if G > 1:
        same = (row // ls) == (col // ls)
        lower = lower & same
        upper = upper & same
        seg_c = lax.broadcasted_iota(I32, (L, 1), 0) // ls

    def per_seq(vals):
        out = vals[0]
        for g in range(1, G):
            out = jnp.where(seg_c == g, vals[g], out)
        return out

    b_c = jnp.sum(jnp.where(lower, lf_r, 0.0), axis=1, keepdims=True)
    b_r = jnp.sum(jnp.where(upper, lf_c, 0.0), axis=0, keepdims=True)
    dmat = jnp.where(lower, b_c - b_r + ig_r, -jnp.inf)
    inter = b_c + per_seq([st[2] for st in states])
    m_new = jnp.maximum(inter, jnp.max(dmat, axis=1, keepdims=True))
    w_intra = jnp.exp(dmat - m_new)
    w_inter = jnp.exp(inter - m_new)
    s = lax.dot_general(q, k, (((1,), (1,)), ((), ())), preferred_element_type=F32) * w_intra
    qf = q.astype(F32)
    qc = [jnp.dot(q, st[0].astype(BF16), preferred_element_type=F32) for st in states]
    qn = [jnp.sum(qf * st[1], axis=1, keepdims=True) for st in states]
    if G > 1:
        q_c = qc[0]
        for g in range(1, G):
            q_c = jnp.where(seg_c == g, qc[g], q_c)
        q_n = per_seq(qn)
    else:
        q_c, q_n = qc[0], qn[0]
    num = jnp.dot(s.astype(BF16), v, preferred_element_type=F32) + w_inter * q_c
    den = jnp.sum(s, axis=1, keepdims=True) + w_inter * q_n
    h = num / jnp.maximum(jnp.abs(den), jnp.exp(-m_new))

    m_last = [m_new[(g + 1) * ls - 1:(g + 1) * ls, :] for g in range(G)]
    b_last = [b_c[(g + 1) * ls - 1:(g + 1) * ls, :] for g in range(G)]
    wl_c = jnp.exp(per_seq(b_last) - b_c + ig_c - per_seq(m_last))
    wk = k.astype(F32) * wl_c
    new_states = []
    for g, (C, n, m) in enumerate(states):
        wk_g = wk if G == 1 else jnp.where(seg_c == g, wk, 0.0)
        wl_inter = jnp.exp(b_last[g] + m - m_last[g])
        kv = lax.dot_general(wk_g.astype(BF16), v, (((0,), (0,)), ((), ())),
                             preferred_element_type=F32)
        new_states.append((wl_inter * C + kv,
                           wl_inter * n + jnp.sum(wk_g, axis=0, keepdims=True),
                           m_last[g]))
    return h, new_states


def _head_out(h, gh, o):
    return _rmsnorm(h, gh) * jax.nn.sigmoid(o.astype(F32))


def _gate_cols(g_c, g_r, bc_ref, br_ref):
    pre_c = g_c + br_ref[...]
    pre_r = g_r + bc_ref[...]
    return pre_c, _log_sigmoid(pre_c), pre_r, _log_sigmoid(pre_r)


def _mlstm_prompt_kernel(q_ref, k_ref, v_ref, o_ref, gc_ref, gr_ref, bc_ref, br_ref, gh_ref,
                         hb_ref, C_out, n_out, m_out, C_s, n_s, m_s, *, nh, d):
    c = pl.program_id(1)

    @pl.when(c == 0)
    def _():
        C_s[...] = jnp.zeros(C_s.shape, F32)
        n_s[...] = jnp.zeros(n_s.shape, F32)
        m_s[...] = jnp.zeros(m_s.shape, F32)

    L = q_ref.shape[0]
    pre_c, lf_call, pre_r, lf_rall = _gate_cols(gc_ref[...], gr_ref[...], bc_ref, br_ref)
    for hd in range(nh):
        cols = slice(hd * d, (hd + 1) * d)
        st = (C_s[hd], n_s[hd], m_s[hd])
        h, (new,) = _mlstm_cell(
            q_ref[:, cols], k_ref[:, cols], v_ref[:, cols],
            pre_c[:, hd:hd + 1], pre_r[hd:hd + 1, :],
            lf_call[:, nh + hd:nh + hd + 1], lf_rall[nh + hd:nh + hd + 1, :],
            [st], L)
        C_s[hd], n_s[hd], m_s[hd] = new
        hb_ref[:, cols] = _head_out(h, gh_ref[:, cols], o_ref[:, cols]).astype(hb_ref.dtype)

    @pl.when(c == pl.num_programs(1) - 1)
    def _():
        C_out[0] = C_s[...]
        n_out[0] = n_s[...]
        m_out[0] = m_s[...]


def _mlstm_prompt(qkvo, gates_c, gates_r, b_if, g_head, B, S, nh, chunk=256):
    W = qkvo.shape[1] // 4
    d = W // nh
    L = _tile(S, chunk)
    nc = S // L
    tok = lambda j: pl.BlockSpec((L, W), lambda b, c, j=j: (b * nc + c, j))
    const = lambda shape: pl.BlockSpec(shape, lambda b, c: (0,) * len(shape))
    return pl.pallas_call(
        functools.partial(_mlstm_prompt_kernel, nh=nh, d=d),
        grid=(B, nc),
        in_specs=[tok(0), tok(1), tok(2), tok(3),
                  pl.BlockSpec((L, 2 * nh), lambda b, c: (b * nc + c, 0)),
                  pl.BlockSpec((2 * nh, L), lambda b, c: (0, b * nc + c)),
                  const((2 * nh, 1)), const((1, 2 * nh)), const((1, W))],
        out_specs=[pl.BlockSpec((L, W), lambda b, c: (b * nc + c, 0)),
                   pl.BlockSpec((1, nh, d, d), lambda b, c: (b, 0, 0, 0)),
                   pl.BlockSpec((1, nh, 1, d), lambda b, c: (b, 0, 0, 0)),
                   pl.BlockSpec((1, nh, 1, 1), lambda b, c: (b, 0, 0, 0))],
        out_shape=[jax.ShapeDtypeStruct((B * S, W), BF16),
                   jax.ShapeDtypeStruct((B, nh, d, d), F32),
                   jax.ShapeDtypeStruct((B, nh, 1, d), F32),
                   jax.ShapeDtypeStruct((B, nh, 1, 1), F32)],
        scratch_shapes=[pltpu.VMEM((nh, d, d), F32), pltpu.VMEM((nh, 1, d), F32),
                        pltpu.VMEM((nh, 1, 1), F32)],
        compiler_params=_cparams(2),
        name="mlstm_prompt",
    )(qkvo, qkvo, qkvo, qkvo, gates_c, gates_r, b_if.reshape(2 * nh, 1), b_if.reshape(1, 2 * nh),
      g_head.reshape(1, W))


def _mlstm_sample_kernel(q_ref, k_ref, v_ref, o_ref, gc_ref, gr_ref, bc_ref, br_ref, gh_ref,
                         C_hbm, n_in, m_in, hb_ref, C_out, n_out, m_out, cbuf, csem,
                         *, nh, d, bb, sq):
    i = pl.program_id(0)
    n_steps = pl.num_programs(0)

    def c_copy(step, slot):
        return pltpu.make_async_copy(C_hbm.at[pl.ds(step * bb, bb)], cbuf.at[slot], csem.at[slot])

    @pl.when(i == 0)
    def _():
        c_copy(0, 0).start()

        @pl.when(n_steps > 1)
        def _():
            c_copy(1, 1).start()

    @pl.when(i + 2 < n_steps)
    def _():
        c_copy(i + 2, (i + 2) % 3).start()

    c_copy(i, i % 3).wait()
    C_in = cbuf.at[i % 3]
    pre_c, lf_call, pre_r, lf_rall = _gate_cols(gc_ref[...], gr_ref[0], bc_ref, br_ref)
    for hd in range(nh):
        cols = slice(hd * d, (hd + 1) * d)
        states = [(C_in[j, hd], n_in[j, hd], m_in[j, hd]) for j in range(bb)]
        h, new = _mlstm_cell(
            q_ref[:, cols], k_ref[:, cols], v_ref[:, cols],
            pre_c[:, hd:hd + 1], pre_r[hd:hd + 1, :],
            lf_call[:, nh + hd:nh + hd + 1], lf_rall[nh + hd:nh + hd + 1, :],
            states, sq)
        for j in range(bb):
            C_out[j, hd], n_out[j, hd], m_out[j, hd] = new[j]
        hb_ref[:, cols] = _head_out(h, gh_ref[:, cols], o_ref[:, cols]).astype(hb_ref.dtype)


def _mlstm_sample(qkvo, gates_c, gates_r, b_if, g_head, C0, n0, m0, row0, Bs, sq, nh, bb=2):
    W = qkvo.shape[1] // 4
    d = W // nh
    L = bb * sq
    blk0 = row0 // L
    tok = lambda j: pl.BlockSpec((L, W), lambda i, j=j: (blk0 + i, j))
    const = lambda shape: pl.BlockSpec(shape, lambda i: (0,) * len(shape))
    st = lambda a, b: pl.BlockSpec((bb, nh, a, b), lambda i: (i, 0, 0, 0))
    return pl.pallas_call(
        functools.partial(_mlstm_sample_kernel, nh=nh, d=d, bb=bb, sq=sq),
        grid=(Bs // bb,),
        in_specs=[tok(0), tok(1), tok(2), tok(3),
                  pl.BlockSpec((L, 2 * nh), lambda i: (blk0 + i, 0)),
                  pl.BlockSpec((1, 2 * nh, L), lambda i: (i, 0, 0)),
                  const((2 * nh, 1)), const((1, 2 * nh)), const((1, W)),
                  pl.BlockSpec(memory_space=pl.ANY), st(1, d), st(1, 1)],
        out_specs=[pl.BlockSpec((L, W), lambda i: (i, 0)), st(d, d), st(1, d), st(1, 1)],
        out_shape=[jax.ShapeDtypeStruct((Bs * sq, W), BF16),
                   jax.ShapeDtypeStruct((Bs, nh, d, d), F32),
                   jax.ShapeDtypeStruct((Bs, nh, 1, d), F32),
                   jax.ShapeDtypeStruct((Bs, nh, 1, 1), F32)],
        scratch_shapes=[pltpu.VMEM((3, bb, nh, d, d), F32), pltpu.SemaphoreType.DMA((3,))],
        compiler_params=_cparams(1),
        name="mlstm_sample",
    )(qkvo, qkvo, qkvo, qkvo, gates_c, gates_r, b_if.reshape(2 * nh, 1), b_if.reshape(1, 2 * nh),
      g_head.reshape(1, W), C0, n0.reshape(Bs, nh, 1, d), m0.reshape(Bs, nh, 1, 1))


def _merge_kernel(ap_ref, as_ref, hp_ref, hs_ref, ga_ref, gb_ref, wa_ref, wb_ref, o_ref,
                  wa_bf, wb_bf, *, n_first):
    m = pl.program_id(1)

    @pl.when(m == 0)
    def _():
        wa_bf[...] = wa_ref[...].astype(BF16)
        wb_bf[...] = wb_ref[...].astype(BF16)

    def body(a, hb):
        pa = jnp.dot(a, wa_bf[...], preferred_element_type=F32)
        pb = jnp.dot(hb, wb_bf[...], preferred_element_type=F32)
        mix = (jax.nn.sigmoid(ga_ref[...].astype(F32)) * pa
               + jax.nn.sigmoid(gb_ref[...].astype(F32)) * pb)
        o_ref[...] = mix.astype(o_ref.dtype)

    @pl.when(m < n_first)
    def _():
        body(ap_ref[...], hp_ref[...])

    @pl.when(m >= n_first)
    def _():
        body(as_ref[...], hs_ref[...])


def _merge(a_p, a_s, hb_p, hb_s, gab, wa, wb, tm=512, tn=1024):
    Tp, P = a_p.shape
    T = Tp + a_s.shape[0]
    W = hb_p.shape[1]
    D = wa.shape[1]
    tm = _tile(math.gcd(Tp, T - Tp), tm)
    tn = _tile(D, tn)
    nn = D // tn
    n_first = Tp // tm
    a_specs = _split_specs((tm, P), n_first, lambda n, m: m)
    h_specs = _split_specs((tm, W), n_first, lambda n, m: m)
    return pl.pallas_call(
        functools.partial(_merge_kernel, n_first=n_first),
        grid=(nn, T // tm),
        in_specs=[*a_specs, *h_specs,
                  pl.BlockSpec((tm, tn), lambda n, m: (m, n)),
                  pl.BlockSpec((tm, tn), lambda n, m: (m, nn + n)),
                  pl.BlockSpec((P, tn), lambda n, m: (0, n)),
                  pl.BlockSpec((W, tn), lambda n, m: (0, n))],
        out_specs=pl.BlockSpec((tm, tn), lambda n, m: (m, n)),
        out_shape=jax.ShapeDtypeStruct((T, D), BF16),
        scratch_shapes=[pltpu.VMEM((P, tn), BF16), pltpu.VMEM((W, tn), BF16)],
        compiler_params=_cparams(2),
        name="merge",
    )(a_p, a_s, hb_p, hb_s, gab, gab, wa, wb)


def _moe_local_rows(tt):
    return -(-(2 * tt + N_EXPERTS * (SUBLANES - 1)) // LANES) * LANES


def _router_kernel(x_ref, g_ref, w_ref, b_ref, xl_ref, meta_ref, cnt_ref, *, lr):
    tt = x_ref.shape[0]
    xn = _rmsnorm(x_ref[...], g_ref[...])
    logits = _dot_split(xn, w_ref[...]) + b_ref[...]
    lane = lax.broadcasted_iota(I32, logits.shape, 1).astype(F32)
    neg = -jnp.inf
    gl = jnp.where(lane < N_EXPERT_GROUPS, logits, neg)
    gmax = jnp.max(gl, axis=1, keepdims=True)
    g_idx = jnp.min(jnp.where(gl == gmax, lane, float(LANES)), axis=1, keepdims=True)
    g_val = 1.0 / jnp.sum(jnp.exp(gl - gmax), axis=1, keepdims=True)
    lo = N_EXPERT_GROUPS + g_idx * EXPERTS_PER_GROUP
    el = jnp.where((lane >= lo) & (lane < lo + EXPERTS_PER_GROUP), logits, neg)
    t1 = jnp.max(el, axis=1, keepdims=True)
    i1 = jnp.min(jnp.where(el == t1, lane, float(LANES)), axis=1, keepdims=True)
    el2 = jnp.where(lane == i1, neg, el)
    t2 = jnp.max(el2, axis=1, keepdims=True)
    i2 = jnp.min(jnp.where(el2 == t2, lane, float(LANES)), axis=1, keepdims=True)
    e2 = jnp.exp(t2 - t1)
    w1 = g_val / (1.0 + e2)
    w2 = g_val * e2 / (1.0 + e2)

    oh1 = (lane == i1 - N_EXPERT_GROUPS).astype(F32)
    oh2 = (lane == i2 - N_EXPERT_GROUPS).astype(F32)
    r_i = lax.broadcasted_iota(I32, (tt, tt), 0)
    c_i = lax.broadcasted_iota(I32, (tt, tt), 1)
    before = (c_i < r_i).astype(BF16)
    rank1 = jnp.dot(before, oh1.astype(BF16), preferred_element_type=F32)
    rank2 = jnp.dot(before, oh2.astype(BF16), preferred_element_type=F32)
    cnt1 = jnp.sum(oh1, axis=0, keepdims=True)
    cnt = cnt1 + jnp.sum(oh2, axis=0, keepdims=True)
    cnt8 = jnp.floor((cnt + (SUBLANES - 1)) * (1.0 / SUBLANES)) * SUBLANES
    e_r = lax.broadcasted_iota(I32, (LANES, LANES), 0)
    e_c = lax.broadcasted_iota(I32, (LANES, LANES), 1)
    start = jnp.dot(jnp.broadcast_to(cnt8, (2 * SUBLANES, LANES)).astype(BF16),
                    (e_r < e_c).astype(BF16), preferred_element_type=F32)[0:1]
    row1 = jnp.sum(oh1 * (start + rank1), axis=1, keepdims=True)
    row2 = jnp.sum(oh2 * (start + cnt1 + rank2), axis=1, keepdims=True)
    meta = jnp.where(lane == 0.0, row1, jnp.where(lane == 1.0, row2,
                     jnp.where(lane == 2.0, w1, jnp.where(lane == 3.0, w2, 0.0))))
    meta_ref[...] = meta
    cnt_ref[0] = cnt8.astype(I32)

    meta_t = meta.T
    dst = lax.broadcasted_iota(I32, (lr, tt), 0).astype(F32)
    place = ((dst == meta_t[0:1, :]) | (dst == meta_t[1:2, :])).astype(BF16)
    xl_ref[...] = jnp.dot(place, xn.astype(BF16), preferred_element_type=F32)


def _router(x1, g_ffn, w_r, b_r, tt):
    T, D = x1.shape
    lr = _moe_local_rows(tt)
    nj = T // tt
    return pl.pallas_call(
        functools.partial(_router_kernel, lr=lr),
        grid=(nj,),
        in_specs=[pl.BlockSpec((tt, D), lambda i: (i, 0)),
                  pl.BlockSpec((1, D), lambda i: (0, 0)),
                  pl.BlockSpec((D, 2 * LANES), lambda i: (0, 0)),
                  pl.BlockSpec((1, LANES), lambda i: (0, 0))],
        out_specs=[pl.BlockSpec((lr, D), lambda i: (i, 0)),
                   pl.BlockSpec((tt, LANES), lambda i: (i, 0)),
                   pl.BlockSpec((1, 1, LANES), lambda i: (i, 0, 0))],
        out_shape=[jax.ShapeDtypeStruct((nj * lr, D), F32),
                   jax.ShapeDtypeStruct((T, LANES), F32),
                   jax.ShapeDtypeStruct((nj, 1, LANES), I32)],
        compiler_params=_cparams(1),
        name="moe_route",
    )(x1, g_ffn.reshape(1, D), w_r, b_r)


def _moe_plan(cnt, tt, lr, tr):
    nj, E = cnt.shape
    n_tiles = nj * (2 * tt + E * (SUBLANES - 1)) // tr + E
    pre = jnp.cumsum(cnt, axis=0) - cnt
    tot = jnp.sum(cnt, axis=0)
    lstart = jnp.cumsum(cnt, axis=1) - cnt
    nt_e = (tot + tr - 1) // tr
    t_end = jnp.cumsum(nt_e)
    t_start = t_end - nt_e
    tile = jnp.arange(n_tiles, dtype=I32)
    tile_e = jnp.minimum(jnp.sum((t_end[None, :] <= tile[:, None]).astype(I32), axis=1), E - 1)
    tile_a = (tile - t_start[tile_e]) * tr
    tile_next = t_end[tile_e]
    n_used = t_end[-1].reshape(1)
    src_local = jnp.arange(nj, dtype=I32)[:, None] * lr + lstart
    src_global = t_start[None, :] * tr + pre
    as_i32 = lambda a: a.astype(I32)
    by_expert = lambda a: as_i32(a.T.reshape(-1))
    by_tile = lambda a: as_i32(a.reshape(-1))
    return dict(tile_e=as_i32(tile_e), tile_a=as_i32(tile_a), tile_next=as_i32(tile_next),
                n_used=as_i32(n_used),
                pre_e=by_expert(pre), cnt_e=by_expert(cnt), src_e=by_expert(src_local),
                cnt_j=by_tile(cnt), dst_j=by_tile(lstart), src_j=by_tile(src_global),
                n_tiles=n_tiles)


def _experts_kernel(te_ref, ta_ref, tn_ref, nu_ref, pre_ref, cnt_ref, src_ref,
                    xl_hbm, wg_hbm, wu_hbm, wd_hbm, o_ref,
                    xbuf, wg_st, wu_st, wd_st, wg_bf, wu_bf, wd_bf, sem, wsem, *, nj, tr):
    i = pl.program_id(0)
    n_used = nu_ref[0]

    def weight_copies(e):
        return (pltpu.make_async_copy(wg_hbm.at[e], wg_st, wsem.at[0]),
                pltpu.make_async_copy(wu_hbm.at[e], wu_st, wsem.at[1]),
                pltpu.make_async_copy(wd_hbm.at[e], wd_st, wsem.at[2]))

    def for_each_run(tile, slot, fn):
        e = te_ref[tile]
        a = ta_ref[tile]

        def body(j, c):
            lo = pre_ref[e * nj + j]
            first = jnp.maximum(lo, a)
            n = jnp.minimum(lo + cnt_ref[e * nj + j], a + tr) - first

            @pl.when(n > 0)
            def _():
                src = pl.multiple_of(src_ref[e * nj + j] + (first - lo), SUBLANES)
                dst = pl.multiple_of(first - a, SUBLANES)
                rows = pl.multiple_of(n, SUBLANES)
                fn(pltpu.make_async_copy(xl_hbm.at[pl.ds(src, rows)],
                                         xbuf.at[slot, pl.ds(dst, rows)], sem.at[slot]))
            return c

        lax.fori_loop(0, nj, body, 0)

    @pl.when((i == 0) & (n_used > 0))
    def _():
        xbuf[...] = jnp.zeros(xbuf.shape, xbuf.dtype)
        for_each_run(0, 0, lambda cp: cp.start())
        for cp in weight_copies(te_ref[0]):
            cp.start(priority=WEIGHT_DMA_PRIORITY)

    @pl.when(i + 1 < n_used)
    def _():
        for_each_run(i + 1, (i + 1) % 2, lambda cp: cp.start())

    @pl.when(i < n_used)
    def _():
        @pl.when((i == 0) | (te_ref[i] != te_ref[jnp.maximum(i - 1, 0)]))
        def _():
            for cp in weight_copies(te_ref[i]):
                cp.wait()
            wg_bf[...] = wg_st[...].astype(BF16)
            wu_bf[...] = wu_st[...].astype(BF16)
            wd_bf[...] = wd_st[...].astype(BF16)
            nxt = tn_ref[i]

            @pl.when(nxt < n_used)
            def _():
                for cp in weight_copies(te_ref[nxt]):
                    cp.start(priority=WEIGHT_DMA_PRIORITY)

        for_each_run(i, i % 2, lambda cp: cp.wait())
        x = xbuf[i % 2].astype(BF16)
        hg = jnp.dot(x, wg_bf[...], preferred_element_type=F32)
        hu = jnp.dot(x, wu_bf[...], preferred_element_type=F32)
        h = hg * jax.nn.sigmoid(hg) * hu
        o_ref[...] = jnp.dot(h.astype(BF16), wd_bf[...], preferred_element_type=F32)

    @pl.when(i >= n_used)
    def _():
        o_ref[...] = jnp.zeros(o_ref.shape, o_ref.dtype)


def _experts(xl, plan, w_eg, w_eu, w_ed, nj, tr):
    D = xl.shape[1]
    E, _, FF = w_eg.shape
    n_tiles = plan["n_tiles"]
    any_space = pl.BlockSpec(memory_space=pl.ANY)
    return pl.pallas_call(
        functools.partial(_experts_kernel, nj=nj, tr=tr),
        grid_spec=pltpu.PrefetchScalarGridSpec(
            num_scalar_prefetch=7,
            grid=(n_tiles,),
            in_specs=[any_space, any_space, any_space, any_space],
            out_specs=pl.BlockSpec((tr, D), lambda i, *_: (i, 0)),
            scratch_shapes=[pltpu.VMEM((2, tr, D), F32),
                            pltpu.VMEM((D, FF), F32), pltpu.VMEM((D, FF), F32),
                            pltpu.VMEM((FF, D), F32),
                            pltpu.VMEM((D, FF), BF16), pltpu.VMEM((D, FF), BF16),
                            pltpu.VMEM((FF, D), BF16),
                            pltpu.SemaphoreType.DMA((2,)), pltpu.SemaphoreType.DMA((3,))]),
        out_shape=jax.ShapeDtypeStruct((n_tiles * tr, D), F32),
        compiler_params=_cparams(1),
        name="moe_experts",
    )(plan["tile_e"], plan["tile_a"], plan["tile_next"], plan["n_used"], plan["pre_e"],
      plan["cnt_e"], plan["src_e"], xl, w_eg, w_eu, w_ed)


def _combine_kernel(cnt_ref, dst_ref, src_ref, x1_ref, meta_ref, g_ref, ys_hbm, yp_ref, ys_ref,
                    ybuf, sem, *, n_first):
    j = pl.program_id(0)
    nj = pl.num_programs(0)
    lr = ybuf.shape[1]

    def for_each_run(tile, slot, fn):
        def body(e, c):
            n = cnt_ref[tile * N_EXPERTS + e]

            @pl.when(n > 0)
            def _():
                src = pl.multiple_of(src_ref[tile * N_EXPERTS + e], SUBLANES)
                dst = pl.multiple_of(dst_ref[tile * N_EXPERTS + e], SUBLANES)
                rows = pl.multiple_of(n, SUBLANES)
                fn(pltpu.make_async_copy(ys_hbm.at[pl.ds(src, rows)],
                                         ybuf.at[slot, pl.ds(dst, rows)], sem.at[slot]))
            return c

        lax.fori_loop(0, N_EXPERTS, body, 0)

    @pl.when(j == 0)
    def _():
        ybuf[...] = jnp.zeros(ybuf.shape, ybuf.dtype)
        for_each_run(0, 0, lambda cp: cp.start())

    @pl.when(j + 1 < nj)
    def _():
        for_each_run(j + 1, (j + 1) % 2, lambda cp: cp.start())

    for_each_run(j, j % 2, lambda cp: cp.wait())
    meta = meta_ref[...]
    col = lax.broadcasted_iota(I32, (meta.shape[0], lr), 1).astype(F32)
    pick = jnp.where(col == meta[:, 0:1], meta[:, 2:3],
                     jnp.where(col == meta[:, 1:2], meta[:, 3:4], 0.0))
    moe = jnp.dot(pick.astype(BF16), ybuf[j % 2].astype(BF16), preferred_element_type=F32)
    y = _rmsnorm(x1_ref[...] + moe, g_ref[...])

    @pl.when(j < n_first)
    def _():
        yp_ref[...] = y

    @pl.when(j >= n_first)
    def _():
        ys_ref[...] = y


def _combine(x1, meta, g_final, ys, plan, Tp, tt, lr):
    T, D = x1.shape
    nj = T // tt
    n_first = Tp // tt
    out_p, out_s = _split_specs((tt, D), n_first, lambda j, *_: j)
    return pl.pallas_call(
        functools.partial(_combine_kernel, n_first=n_first),
        grid_spec=pltpu.PrefetchScalarGridSpec(
            num_scalar_prefetch=3,
            grid=(nj,),
            in_specs=[pl.BlockSpec((tt, D), lambda j, *_: (j, 0)),
                      pl.BlockSpec((tt, LANES), lambda j, *_: (j, 0)),
                      pl.BlockSpec((1, D), lambda j, *_: (0, 0)),
                      pl.BlockSpec(memory_space=pl.ANY)],
            out_specs=[out_p, out_s],
            scratch_shapes=[pltpu.VMEM((2, lr, D), F32), pltpu.SemaphoreType.DMA((2,))]),
        out_shape=[jax.ShapeDtypeStruct((Tp, D), F32), jax.ShapeDtypeStruct((T - Tp, D), F32)],
        compiler_params=_cparams(1),
        name="moe_combine",
    )(plan["cnt_j"], plan["dst_j"], plan["src_j"], x1, meta, g_final.reshape(1, D), ys)


def _pad_cols(w, n):
    return jnp.pad(w, ((0, 0), (0, n - w.shape[1])))


def _layer(xp, xs, Bp, Sp, Bs, Ss, state_pool, state_C, state_n, state_m, g_mix, w_in, b_if,
           w_pool, pool_scale, w_proj_a, w_proj_b, g_head, w_out, g_ffn, w_rg, b_rg, w_re, b_re,
           w_eg, w_eu, w_ed, g_out):
    D = xp.shape[1]
    Tp = Bp * Sp
    T = Tp + Bs * Ss
    P = w_pool.shape[0] * w_pool.shape[1]
    nh = b_if.shape[0] // 2
    W = w_proj_b.shape[0]
    d = W // nh
    gate0 = P + 4 * W

    w_in_t = w_in.T
    xn, gates = _rms_proj(xp, xs, g_mix, w_in_t[gate0:gate0 + 2 * nh].T)
    gates = gates[:, :2 * nh]
    u = _matmul_wt(xn, w_in_t, 0, P, F32)
    k_scale = jnp.concatenate([jnp.ones((W,), F32), jnp.full((W,), d ** -0.5, F32),
                               jnp.ones((2 * W,), F32)])
    qkvo = _matmul_wt(xn, w_in_t, P, 4 * W, BF16, scale=k_scale)
    gab = _matmul_wt(xn, w_in_t, gate0 + 2 * nh, 2 * D, BF16)

    wp_bf = w_pool.astype(BF16)
    nbuf = state_pool.shape[1]
    hist = jnp.pad(state_pool, ((0, 0), (POOL_HIST - nbuf, 0), (0, 0)))
    a_p = _pool_prompt(u, wp_bf, pool_scale, Bp, Sp)
    a_s = _pool_sample(u, hist, wp_bf, pool_scale, Tp, Bs, Ss)
    pool_p = jnp.stack([u[(b + 1) * Sp - nbuf:(b + 1) * Sp] for b in range(Bp)])
    pool_s = jnp.concatenate([state_pool, u[Tp:].reshape(Bs, Ss, P)], axis=1)[:, -nbuf:]

    gates_r = gates.T
    hb_p, C_p, n_p, m_p = _mlstm_prompt(qkvo, gates, gates_r, b_if, g_head, Bp, Sp, nh)
    bb = 2
    gr_s = gates_r[:, Tp:].reshape(2 * nh, Bs // bb, bb * Ss).transpose(1, 0, 2)
    hb_s, C_s, n_s, m_s = _mlstm_sample(qkvo, gates, gr_s, b_if, g_head, state_C, state_n,
                                        state_m, Tp, Bs, Ss, nh, bb=bb)

    mix = _merge(a_p, a_s, hb_p, hb_s, gab, w_proj_a, w_proj_b)
    x1 = _matmul(mix, w_out, F32, res=(xp, xs))

    tt = _tile(math.gcd(Tp, T - Tp), MOE_TT)
    lr = _moe_local_rows(tt)
    w_r = _split_weight(jnp.concatenate([w_rg, w_re], axis=1))
    b_r = _pad_cols(jnp.concatenate([b_rg, b_re]).reshape(1, -1), LANES)
    xl, meta, cnt = _router(x1, g_ffn, w_r, b_r, tt)
    plan = _moe_plan(cnt[:, 0, :N_EXPERTS], tt, lr, MOE_TR)
    ys = _experts(xl, plan, w_eg, w_eu, w_ed, T // tt, MOE_TR)
    y_p, y_s = _combine(x1, meta, g_out, ys, plan, Tp, tt, lr)
    states_p = (pool_p, C_p, n_p.reshape(Bp, nh, d), m_p.reshape(Bp, nh))
    states_s = (pool_s, C_s, n_s.reshape(Bs, nh, d), m_s.reshape(Bs, nh))
    return y_p, y_s, states_p, states_s


def kernel(x_prompt, x_sample, state_pool, state_C, state_n, state_m, g_mix, w_in, b_if, w_pool, pool_scale, w_proj_a, w_proj_b, g_head, w_out, g_ffn, w_router_group, b_router_group, w_router_expert, b_router_expert, w_exp_gate, w_exp_up, w_exp_down, g_final):
    Bp, Sp, D = x_prompt.shape
    Bs, Ss, _ = x_sample.shape
    assert g_mix.shape[0] == 1, "the closing norm is fused into the layer's last kernel"
    l = 0
    y_p, y_s, sp, ss = _layer(
        x_prompt.reshape(Bp * Sp, D), x_sample.reshape(Bs * Ss, D), Bp, Sp, Bs, Ss,
        state_pool[l], state_C[l], state_n[l], state_m[l], g_mix[l], w_in[l], b_if[l], w_pool[l],
        pool_scale[l], w_proj_a[l], w_proj_b[l], g_head[l], w_out[l], g_ffn[l],
        w_router_group[l], b_router_group[l], w_router_expert[l], b_router_expert[l],
        w_exp_gate[l], w_exp_up[l], w_exp_down[l], g_final)
    return (y_p.reshape(Bp, Sp, D), y_s.reshape(Bs, Ss, D),
            sp[0][None], sp[1][None], sp[2][None], sp[3][None],
            ss[0][None], ss[1][None], ss[2][None], ss[3][None])
```
